```python
import math
import jax, jax.numpy as jnp
from jax import lax
import numpy as np

D_MODEL = 1024
BATCH = 32
SEQ = 2048
DEPTH = 1

GRID_W = 64
CTX_LEN = 256
HEAD_DIM = 64
MIX_WIDTH = D_MODEL
DIFF_WIDTH = MIX_WIDTH // 2
NA_WIDTH = MIX_WIDTH - DIFF_WIDTH
DIFF_HEADS = DIFF_WIDTH // (2 * HEAD_DIM)
NA_HEADS = NA_WIDTH // HEAD_DIM
NA_KH_MAX = 8
NA_KW = 16
D_FF = 4 * D_MODEL
ROPE_BASE = 10000.0
ROPE_FREQS = HEAD_DIM // 4
Q_BLOCK = 128
EPS = 1e-6
NEG_INF = -1e30

kernel_name = 'hybrid_diffattn_natten_dit_block'


def rmsnorm(x, g):
    xf = x.astype(jnp.float32)
    y = xf * lax.rsqrt(jnp.mean(xf * xf, axis=-1, keepdims=True) + EPS)
    return (y * g.astype(jnp.float32)).astype(x.dtype)


def modulate(h, shift, scale):
    return h * (1.0 + scale) + shift


def adaln_chunks(cond, w_mod_l, b_mod_l):
    mod = jax.nn.silu(cond) @ w_mod_l + b_mod_l
    return jnp.split(mod, 6, axis=-1)


def squared_relu_mlp(h, w1, w2):
    return jnp.square(jax.nn.relu(h @ w1)) @ w2


def axial_rope_tables(rows, cols):
    inv = ROPE_BASE ** (-jnp.arange(ROPE_FREQS, dtype=jnp.float32) / ROPE_FREQS)
    ang = jnp.stack([rows, cols], axis=-1).astype(jnp.float32)[..., None] * inv
    return jnp.cos(ang), jnp.sin(ang)


def apply_axial_rope(x, cos, sin):
    shape = x.shape
    xr = x.reshape(shape[:-1] + (2, 2, ROPE_FREQS))
    bshape = (shape[1],) + (1,) * (x.ndim - 3) + (2, ROPE_FREQS)
    c = cos.reshape(bshape).astype(x.dtype)
    s = sin.reshape(bshape).astype(x.dtype)
    x1 = xr[..., 0, :]
    x2 = xr[..., 1, :]
    out = jnp.stack([x1 * c - x2 * s, x1 * s + x2 * c], axis=-2)
    return out.reshape(shape)


def unpack_q(q):
    B, L = q.shape[:2]
    q_d = q[..., :DIFF_WIDTH].reshape(B, L, DIFF_HEADS, 2, HEAD_DIM)
    q_n = q[..., DIFF_WIDTH:].reshape(B, L, NA_HEADS, HEAD_DIM)
    return q_d, q_n


def unpack_kv(kv):
    B, L = kv.shape[:2]
    k_d = kv[..., :DIFF_WIDTH].reshape(B, L, DIFF_HEADS, 2, HEAD_DIM)
    k_n = kv[..., DIFF_WIDTH:MIX_WIDTH].reshape(B, L, NA_HEADS, HEAD_DIM)
    v_d = kv[..., MIX_WIDTH:MIX_WIDTH + DIFF_WIDTH].reshape(B, L, DIFF_HEADS, 2 * HEAD_DIM)
    v_n = kv[..., MIX_WIDTH + DIFF_WIDTH:].reshape(B, L, NA_HEADS, HEAD_DIM)
    return k_d, k_n, v_d, v_n


def diff_lambda(lq1, lk1, lq2, lk2, lam_init):
    f = jnp.float32
    return (jnp.exp(jnp.sum(lq1.astype(f) * lk1.astype(f)))
            - jnp.exp(jnp.sum(lq2.astype(f) * lk2.astype(f))) + lam_init)


def diff_attend(qi, k, v, lam):
    s = jnp.einsum('bqhid,bkhid->bhiqk', qi, k).astype(jnp.float32) * HEAD_DIM ** -0.5
    p = jax.nn.softmax(s, axis=-1)
    a = p[:, :, 0] - lam * p[:, :, 1]
    return jnp.einsum('bhqk,bkhd->bqhd', a.astype(v.dtype), v)


def diff_head_norm(o, g, lam_init):
    return rmsnorm(o, g) * (1.0 - lam_init)


def diff_attention_latent(q, k, v, k_ctx, v_ctx, lam):
    B, S = q.shape[:2]
    keys = jnp.concatenate([k, k_ctx], axis=1)
    vals = jnp.concatenate([v, v_ctx], axis=1)
    nb = S // Q_BLOCK
    qb = jnp.moveaxis(q.reshape((B, nb, Q_BLOCK) + q.shape[2:]), 1, 0)
    o = lax.map(lambda qi: diff_attend(qi, keys, vals, lam), qb)
    return jnp.moveaxis(o, 0, 1).reshape(B, S, DIFF_HEADS, 2 * HEAD_DIM)


def softmax_attention(q, k, v):
    s = jnp.einsum('bqhd,bkhd->bhqk', q, k).astype(jnp.float32) * HEAD_DIM ** -0.5
    p = jax.nn.softmax(s, axis=-1)
    return jnp.einsum('bhqk,bkhd->bqhd', p.astype(v.dtype), v)


def neighbourhood_attention_latent(q, k, v, k_ctx, v_ctx, rpb):
    B, S, H, dh = q.shape
    rows = S // GRID_W
    kh = min(NA_KH_MAX, rows)
    qg = q.reshape(B, rows, GRID_W, H, dh)
    kg = k.reshape(B, rows, GRID_W, H, dh)
    vg = v.reshape(B, rows, GRID_W, H, dh)
    jq = jnp.arange(GRID_W)
    col_start = jnp.clip(jq - NA_KW // 2, 0, GRID_W - NA_KW)
    in_win = (jq[None, :] >= col_start[:, None]) & (jq[None, :] < col_start[:, None] + NA_KW)
    dc_idx = jnp.clip(jq[None, :] - jq[:, None] + NA_KW - 1, 0, 2 * NA_KW - 2)
    kr = jnp.arange(kh)
    scale = HEAD_DIM ** -0.5

    def row_block(r):
        start = jnp.clip(r - kh // 2, 0, rows - kh)
        q_r = lax.dynamic_index_in_dim(qg, r, axis=1, keepdims=False)
        k_s = lax.dynamic_slice_in_dim(kg, start, kh, axis=1)
        v_s = lax.dynamic_slice_in_dim(vg, start, kh, axis=1)
        s_lat = jnp.einsum('bqhd,bxkhd->bhqxk', q_r, k_s).astype(jnp.float32) * scale
        dr_idx = start + kr - r + NA_KH_MAX - 1
        bias = jnp.transpose(rpb[:, dr_idx][:, :, dc_idx], (0, 2, 1, 3))
        s_lat = jnp.where(in_win[:, None, :], s_lat + bias.astype(jnp.float32), NEG_INF)
        s_lat = s_lat.reshape(B, H, GRID_W, kh * GRID_W)
        s_ctx = jnp.einsum('bqhd,bchd->bhqc', q_r, k_ctx).astype(jnp.float32) * scale
        p = jax.nn.softmax(jnp.concatenate([s_lat, s_ctx], axis=-1), axis=-1).astype(v.dtype)
        n_lat = kh * GRID_W
        o = (jnp.einsum('bhqn,bnhd->bqhd', p[..., :n_lat], v_s.reshape(B, n_lat, H, dh))
             + jnp.einsum('bhqc,bchd->bqhd', p[..., n_lat:], v_ctx))
        return o

    o = lax.map(row_block, jnp.arange(rows))
    return jnp.moveaxis(o, 0, 1).reshape(B, S, H, dh)


def setup_inputs(seed: int = 0) -> dict:
    key = jax.random.key(seed)
    ks = jax.random.split(key, 19)

    def nrm(k, shape, s):
        return s * jax.random.normal(k, shape, jnp.float32)

    return {
        'x': nrm(ks[0], (BATCH, SEQ, D_MODEL), 1.0),
        'c': nrm(ks[1], (BATCH, D_MODEL), 1.0),
        'ctx': nrm(ks[2], (BATCH, CTX_LEN, D_MODEL), 1.0),
        'c_ctx': nrm(ks[3], (D_MODEL,), 1.0),
        'w_mod': nrm(ks[4], (DEPTH, D_MODEL, 6 * D_MODEL), 0.5 * D_MODEL ** -0.5),
        'b_mod': nrm(ks[5], (DEPTH, 6 * D_MODEL), 0.02),
        'norm1_g': 1.0 + nrm(ks[6], (DEPTH, D_MODEL), 0.02),
        'w_in': nrm(ks[7], (DEPTH, D_MODEL, 3 * MIX_WIDTH), D_MODEL ** -0.5),
        'lam_q1': nrm(ks[8], (DEPTH, HEAD_DIM), 0.1),
        'lam_k1': nrm(ks[9], (DEPTH, HEAD_DIM), 0.1),
        'lam_q2': nrm(ks[10], (DEPTH, HEAD_DIM), 0.1),
        'lam_k2': nrm(ks[11], (DEPTH, HEAD_DIM), 0.1),
        'diff_subln_g': 1.0 + nrm(ks[12], (DEPTH, 2 * HEAD_DIM), 0.02),
        'na_rpb': nrm(ks[13], (DEPTH, NA_HEADS, 2 * NA_KH_MAX - 1, 2 * NA_KW - 1), 0.1),
        'w_out': nrm(ks[14], (DEPTH, MIX_WIDTH, D_MODEL), MIX_WIDTH ** -0.5),
        'norm2_g': 1.0 + nrm(ks[15], (DEPTH, D_MODEL), 0.02),
        'w_fc1': nrm(ks[16], (DEPTH, D_MODEL, D_FF), D_MODEL ** -0.5),
        'w_fc2': nrm(ks[17], (DEPTH, D_FF, D_MODEL), D_FF ** -0.5),
        'final_g': 1.0 + nrm(ks[18], (D_MODEL,), 0.02),
    }


def reference(x, c, ctx, c_ctx, w_mod, b_mod, norm1_g, w_in, lam_q1, lam_k1, lam_q2, lam_k2,
              diff_subln_g, na_rpb, w_out, norm2_g, w_fc1, w_fc2, final_g):
    B, S, _ = x.shape
    pos = jnp.arange(S)
    cos, sin = axial_rope_tables(pos // GRID_W, pos % GRID_W)
    for l in range(DEPTH):
        lam_init = 0.8 - 0.6 * math.exp(-0.3 * l)
        lam = diff_lambda(lam_q1[l], lam_k1[l], lam_q2[l], lam_k2[l], lam_init)
        sh_a, sc_a, g_a, sh_m, sc_m, g_m = [m[:, None, :] for m in adaln_chunks(c, w_mod[l], b_mod[l])]
        csh_a, csc_a, cg_a, csh_m, csc_m, cg_m = adaln_chunks(c_ctx, w_mod[l], b_mod[l])

        h = modulate(rmsnorm(x, norm1_g[l]), sh_a, sc_a)
        hc = modulate(rmsnorm(ctx, norm1_g[l]), csh_a, csc_a)
        proj = h @ w_in[l]
        q_d, q_n = unpack_q(proj[..., :MIX_WIDTH])
        k_d, k_n, v_d, v_n = unpack_kv(proj[..., MIX_WIDTH:])
        kc_d, kc_n, vc_d, vc_n = unpack_kv(hc @ w_in[l, :, MIX_WIDTH:])
        q_d = apply_axial_rope(q_d, cos, sin)
        k_d = apply_axial_rope(k_d, cos, sin)
        o_d = diff_head_norm(diff_attention_latent(q_d, k_d, v_d, kc_d, vc_d, lam),
                             diff_subln_g[l], lam_init)
        o_n = neighbourhood_attention_latent(q_n, k_n, v_n, kc_n, vc_n, na_rpb[l])
        mixed = jnp.concatenate([o_d.reshape(B, S, DIFF_WIDTH), o_n.reshape(B, S, NA_WIDTH)], axis=-1)

        if l < DEPTH - 1:
            qc_d, qc_n = unpack_q(hc @ w_in[l, :, :MIX_WIDTH])
            oc_d = diff_head_norm(diff_attend(qc_d, kc_d, vc_d, lam), diff_subln_g[l], lam_init)
            oc_n = softmax_attention(qc_n, kc_n, vc_n)
            mixed_c = jnp.concatenate([oc_d.reshape(B, CTX_LEN, DIFF_WIDTH),
                                       oc_n.reshape(B, CTX_LEN, NA_WIDTH)], axis=-1)
            ctx = ctx + cg_a * (mixed_c @ w_out[l])
            hc2 = modulate(rmsnorm(ctx, norm2_g[l]), csh_m, csc_m)
            ctx = ctx + cg_m * squared_relu_mlp(hc2, w_fc1[l], w_fc2[l])

        x = x + g_a * (mixed @ w_out[l])
        h2 = modulate(rmsnorm(x, norm2_g[l]), sh_m, sc_m)
        x = x + g_m * squared_relu_mlp(h2, w_fc1[l], w_fc2[l])
    return rmsnorm(x, final_g)
```

```python
import functools
import math

import numpy as np
import jax
import jax.numpy as jnp
from jax import lax
from jax.experimental import pallas as pl
from jax.experimental.pallas import tpu as pltpu

GRID_W = 64
HEAD_DIM = 64
DIFF_HEADS = 4
NA_HEADS = 8
NA_KH = 8
NA_KW = 16
ROPE_BASE = 10000.0
ROPE_FREQS = HEAD_DIM // 4
EPS = 1e-6
NEG_INF = -1e30
LAM_INIT = 0.8 - 0.6 * math.exp(-0.3 * 0)

LANES = 128
SUBLANES = 8
VMEM_LIMIT = 56 * 1024 * 1024

BF16 = jnp.bfloat16
F32 = jnp.float32


def _params(*sem):
    return pltpu.CompilerParams(dimension_semantics=sem, vmem_limit_bytes=VMEM_LIMIT)


def _const_spec(shape):
    nd = len(shape)
    return pl.BlockSpec(shape, lambda *_: (0,) * nd, pipeline_mode=pl.Buffered(1))


def _dot(a, b):
    return jnp.dot(a, b, preferred_element_type=F32)


def _dot_nt(a, b):
    return lax.dot_general(a, b, (((1,), (1,)), ((), ())), preferred_element_type=F32)


def _rms(xf):
    return xf * lax.rsqrt(jnp.mean(xf * xf, axis=-1, keepdims=True) + EPS)


def _mod_kernel(cond_ref, w_ref, b_ref, o_ref):
    cnd = cond_ref[...]
    act = (cnd * jax.nn.sigmoid(cnd)).astype(BF16)
    o_ref[...] = _dot(act, w_ref[...].astype(BF16)) + b_ref[...]


def _mod(cond, w_mod, b_mod, tn=768):
    r, d = cond.shape
    n = w_mod.shape[1]
    return pl.pallas_call(
        _mod_kernel,
        out_shape=jax.ShapeDtypeStruct((r, n), F32),
        grid=(n // tn,),
        in_specs=[pl.BlockSpec((r, d), lambda j: (0, 0)),
                  pl.BlockSpec((d, tn), lambda j: (0, j)),
                  pl.BlockSpec((1, tn), lambda j: (0, j))],
        out_specs=pl.BlockSpec((r, tn), lambda j: (0, j)),
        compiler_params=_params("arbitrary"),
        name="mod",
    )(cond, w_mod, b_mod)


def _rope(xf, cos, sin_signed):
    lane = lax.broadcasted_iota(jnp.int32, xf.shape, 1)
    first = (lane % (2 * ROPE_FREQS)) < ROPE_FREQS
    partner = jnp.where(first,
                        pltpu.roll(xf, LANES - ROPE_FREQS, 1),
                        pltpu.roll(xf, ROPE_FREQS, 1))
    return xf * cos + partner * sin_signed


def _inproj_kernel(x_ref, mod_ref, g_ref, w_ref, cos_ref, sin_ref, *out_refs,
                   n_rope, q_scale, has_q, chunk):
    xf = x_ref[0]
    shift = mod_ref[0, 0:1, :]
    scale = mod_ref[0, 1:2, :]
    h = (_rms(xf) * g_ref[...]) * (1.0 + scale) + shift
    hb = h.astype(BF16)
    width = out_refs[0].shape[-1]
    for oi, o_ref in enumerate(out_refs):
        is_q = has_q and oi == 0
        roped = has_q and oi < 2
        for c0 in range(0, width, chunk):
            p = _dot(hb, w_ref[:, oi * width + c0:oi * width + c0 + chunk])
            for c in range(c0 // LANES, (c0 + chunk) // LANES):
                pc = p[:, c * LANES - c0:(c + 1) * LANES - c0]
                if roped and c < n_rope:
                    pc = _rope(pc, cos_ref[...], sin_ref[...])
                if is_q:
                    pc = pc * q_scale
                o_ref[0, :, c * LANES:(c + 1) * LANES] = pc.astype(o_ref.dtype)


def _inproj(x, mod, g, w, cos, sin, *, ts, has_q):
    b, t, d = x.shape
    n_out = w.shape[1] // d
    per_batch_mod = mod.shape[0] > 1
    kern = functools.partial(_inproj_kernel, n_rope=(DIFF_HEADS if has_q else 0),
                             q_scale=HEAD_DIM ** -0.5, has_q=has_q, chunk=512)
    return pl.pallas_call(
        kern,
        out_shape=[jax.ShapeDtypeStruct((b, t, d), BF16)] * n_out,
        grid=(b, t // ts),
        in_specs=[pl.BlockSpec((1, ts, d), lambda i, j: (i, j, 0)),
                  pl.BlockSpec((1,) + mod.shape[1:],
                               (lambda i, j: (i, 0, 0)) if per_batch_mod else (lambda i, j: (0, 0, 0))),
                  _const_spec(g.shape),
                  _const_spec(w.shape),
                  pl.BlockSpec((ts, LANES), lambda i, j: (j, 0)),
                  pl.BlockSpec((ts, LANES), lambda i, j: (j, 0))],
        out_specs=[pl.BlockSpec((1, ts, d), lambda i, j: (i, j, 0))] * n_out,
        compiler_params=_params("parallel", "arbitrary"),
        name="inproj_q" if has_q else "inproj_ctx",
    )(x, mod, g, w, cos, sin)


def _diff_kernel(q_ref, k_ref, v_ref, kc_ref, vc_ref, lamv_ref, g_ref, o_ref):
    q = q_ref[0]
    lane = lax.broadcasted_iota(jnp.int32, q.shape, 1)
    lv = lamv_ref[...]
    lam = (jnp.exp(jnp.sum(lv[0:1] * lv[1:2], axis=-1, keepdims=True))
           - jnp.exp(jnp.sum(lv[2:3] * lv[3:4], axis=-1, keepdims=True)) + LAM_INIT)
    outs = []
    for i in range(2):
        sel = (lane < HEAD_DIM) if i == 0 else (lane >= HEAD_DIM)
        qi = jnp.where(sel, q, jnp.zeros_like(q))
        s_lat = _dot_nt(qi, k_ref[0])
        s_ctx = _dot_nt(qi, kc_ref[0])
        m = jnp.maximum(jnp.max(s_lat, axis=-1, keepdims=True),
                        jnp.max(s_ctx, axis=-1, keepdims=True))
        e_lat = jnp.exp(s_lat - m)
        e_ctx = jnp.exp(s_ctx - m)
        l = jnp.sum(e_lat, axis=-1, keepdims=True) + jnp.sum(e_ctx, axis=-1, keepdims=True)
        o = _dot(e_lat.astype(BF16), v_ref[0]) + _dot(e_ctx.astype(BF16), vc_ref[0])
        outs.append(o / l)
    o = outs[0] - lam * outs[1]
    o_ref[0] = (_rms(o) * g_ref[...] * (1.0 - LAM_INIT)).astype(o_ref.dtype)


def _diff_attention(q, k, v, kc, vc, lamv, g, *, tq):
    b, s, _ = q.shape
    c = kc.shape[1]
    w = 2 * HEAD_DIM
    return pl.pallas_call(
        _diff_kernel,
        out_shape=jax.ShapeDtypeStruct((b, s, DIFF_HEADS * w), BF16),
        grid=(b, DIFF_HEADS, s // tq),
        in_specs=[pl.BlockSpec((1, tq, w), lambda i, h, j: (i, j, h)),
                  pl.BlockSpec((1, s, w), lambda i, h, j: (i, 0, h)),
                  pl.BlockSpec((1, s, w), lambda i, h, j: (i, 0, h)),
                  pl.BlockSpec((1, c, w), lambda i, h, j: (i, 0, h)),
                  pl.BlockSpec((1, c, w), lambda i, h, j: (i, 0, h)),
                  pl.BlockSpec(lamv.shape, lambda i, h, j: (0, 0)),
                  pl.BlockSpec(g.shape, lambda i, h, j: (0, 0))],
        out_specs=pl.BlockSpec((1, tq, w), lambda i, h, j: (i, j, h)),
        compiler_params=_params("parallel", "arbitrary", "arbitrary"),
        name="diff_attn",
    )(q, k, v, kc, vc, lamv, g)


def _na_start(r, rows):
    return jnp.clip(r - NA_KH // 2, 0, rows - NA_KH)


def _natten_kernel(q_ref, k_ref, v_ref, kc_ref, vc_ref, bias_ref, o_ref, *, rows):
    r = pl.program_id(1)
    start = pl.multiple_of(_na_start(r, rows) * GRID_W, GRID_W)
    n_lat = NA_KH * GRID_W
    lane = lax.broadcasted_iota(jnp.int32, (GRID_W, LANES), 1)
    for hp in range(NA_HEADS // 2):
        cols = slice(hp * LANES, (hp + 1) * LANES)
        qp = q_ref[0, :, cols]
        kw = k_ref[0, pl.ds(start, n_lat), cols]
        vw = v_ref[0, pl.ds(start, n_lat), cols]
        kcp = kc_ref[0, :, cols]
        vcp = vc_ref[0, :, cols]
        halves = []
        for half in range(2):
            sel = (lane < HEAD_DIM) if half == 0 else (lane >= HEAD_DIM)
            qh = jnp.where(sel, qp, jnp.zeros_like(qp))
            s_lat = _dot_nt(qh, kw) + bias_ref[0, 2 * hp + half]
            s_ctx = _dot_nt(qh, kcp)
            m = jnp.maximum(jnp.max(s_lat, axis=-1, keepdims=True),
                            jnp.max(s_ctx, axis=-1, keepdims=True))
            e_lat = jnp.exp(s_lat - m)
            e_ctx = jnp.exp(s_ctx - m)
            l = jnp.sum(e_lat, axis=-1, keepdims=True) + jnp.sum(e_ctx, axis=-1, keepdims=True)
            o = _dot(e_lat.astype(BF16), vw) + _dot(e_ctx.astype(BF16), vcp)
            halves.append(o / l)
        o_ref[0, :, cols] = jnp.where(lane < HEAD_DIM, halves[0], halves[1]).astype(o_ref.dtype)


def _natten(q, k, v, kc, vc, bias):
    b, s, d = q.shape
    c = kc.shape[1]
    rows = s // GRID_W
    w = NA_HEADS * HEAD_DIM

    def bias_idx(i, r):
        return (r - _na_start(r, rows), 0, 0, 0)

    return pl.pallas_call(
        functools.partial(_natten_kernel, rows=rows),
        out_shape=jax.ShapeDtypeStruct((b, s, w), BF16),
        grid=(b, rows),
        in_specs=[pl.BlockSpec((1, GRID_W, w), lambda i, r: (i, r, 1)),
                  pl.BlockSpec((1, s, w), lambda i, r: (i, 0, 1)),
                  pl.BlockSpec((1, s, w), lambda i, r: (i, 0, 1)),
                  pl.BlockSpec((1, c, w), lambda i, r: (i, 0, 1)),
                  pl.BlockSpec((1, c, w), lambda i, r: (i, 0, 1)),
                  pl.BlockSpec((1,) + bias.shape[1:], bias_idx)],
        out_specs=pl.BlockSpec((1, GRID_W, w), lambda i, r: (i, r, 0)),
        compiler_params=_params("parallel", "arbitrary"),
        name="natten",
    )(q, k, v, kc, vc, bias)


def _na_bias_tables(rpb):
    jq = np.arange(GRID_W)
    col_start = np.clip(jq - NA_KW // 2, 0, GRID_W - NA_KW)
    in_win = (jq[None, :] >= col_start[:, None]) & (jq[None, :] < col_start[:, None] + NA_KW)
    dc_idx = np.clip(jq[None, :] - jq[:, None] + NA_KW - 1, 0, 2 * NA_KW - 2)
    kr = np.arange(NA_KH)
    tables = []
    for d in range(NA_KH):
        dr_idx = kr - d + NA_KH - 1
        t = rpb[:, dr_idx][:, :, dc_idx]
        t = jnp.transpose(t, (0, 2, 1, 3))
        t = jnp.where(in_win[None, :, None, :], t, NEG_INF)
        tables.append(t.reshape(NA_HEADS, GRID_W, NA_KH * GRID_W))
    return jnp.stack(tables).astype(F32)


def _tail_kernel(x_ref, od_ref, on_ref, mod_ref, wo_ref, g2_ref, w1_ref, w2_ref, gf_ref, o_ref,
                 *, ff_chunk):
    half = od_ref.shape[-1]
    gate_a = mod_ref[0, 2:3, :]
    shift_m = mod_ref[0, 3:4, :]
    scale_m = mod_ref[0, 4:5, :]
    gate_m = mod_ref[0, 5:6, :]
    attn = _dot(od_ref[0], wo_ref[:half, :]) + _dot(on_ref[0], wo_ref[half:, :])
    x1 = x_ref[0] + gate_a * attn
    h2 = ((_rms(x1) * g2_ref[...]) * (1.0 + scale_m) + shift_m).astype(BF16)
    y = None
    for c in range(w1_ref.shape[1] // ff_chunk):
        cols = slice(c * ff_chunk, (c + 1) * ff_chunk)
        a = jnp.maximum(_dot(h2, w1_ref[:, cols]), 0.0)
        part = _dot((a * a).astype(BF16), w2_ref[cols, :])
        y = part if y is None else y + part
    x2 = x1 + gate_m * y
    o_ref[0] = _rms(x2) * gf_ref[...]


def _tail(x, o_d, o_n, mod, w_out, g2, w1, w2, gf, *, ts, ff_chunk=1024):
    b, s, d = x.shape
    half = o_d.shape[-1]
    return pl.pallas_call(
        functools.partial(_tail_kernel, ff_chunk=ff_chunk),
        out_shape=jax.ShapeDtypeStruct((b, s, d), F32),
        grid=(b, s // ts),
        in_specs=[pl.BlockSpec((1, ts, d), lambda i, j: (i, j, 0)),
                  pl.BlockSpec((1, ts, half), lambda i, j: (i, j, 0)),
                  pl.BlockSpec((1, ts, half), lambda i, j: (i, j, 0)),
                  pl.BlockSpec((1,) + mod.shape[1:], lambda i, j: (i, 0, 0)),
                  _const_spec(w_out.shape),
                  _const_spec(g2.shape),
                  _const_spec(w1.shape),
                  _const_spec(w2.shape),
                  _const_spec(gf.shape)],
        out_specs=pl.BlockSpec((1, ts, d), lambda i, j: (i, j, 0)),
        compiler_params=_params("parallel", "arbitrary"),
        name="tail",
    )(x, o_d, o_n, mod, w_out, g2, w1, w2, gf)


def _rope_tables(s):
    pos = np.arange(s)
    inv = ROPE_BASE ** (-np.arange(ROPE_FREQS, dtype=np.float32) / ROPE_FREQS)
    ang_r = (pos // GRID_W).astype(np.float32)[:, None] * inv
    ang_c = (pos % GRID_W).astype(np.float32)[:, None] * inv
    cos = np.concatenate([np.cos(ang_r)] * 2 + [np.cos(ang_c)] * 2, axis=1)
    sin = np.concatenate([-np.sin(ang_r), np.sin(ang_r), -np.sin(ang_c), np.sin(ang_c)], axis=1)
    reps = LANES // HEAD_DIM
    return (jnp.asarray(np.tile(cos, (1, reps)), F32), jnp.asarray(np.tile(sin, (1, reps)), F32))


def _tiles(s):
    ts = min(512, s)
    tq = min(512, s)
    return ts, tq


def kernel(x, c, ctx, c_ctx, w_mod, b_mod, norm1_g, w_in, lam_q1, lam_k1, lam_q2, lam_k2,
           diff_subln_g, na_rpb, w_out, norm2_g, w_fc1, w_fc2, final_g):
    b, s, d = x.shape
    assert w_mod.shape[0] == 1, "single-layer block"
    assert s % GRID_W == 0 and s // GRID_W >= NA_KH
    ts, tq = _tiles(s)

    n_cond = b + 1
    pad = (-n_cond) % SUBLANES
    cond = jnp.concatenate([c, c_ctx[None, :], jnp.zeros((pad, d), F32)], axis=0)
    mod = _mod(cond, w_mod[0], b_mod[0][None, :])
    mod_x = mod[:b].reshape(b, 6, d)
    mod_c = mod[b:b + 1].reshape(1, 6, d)

    w_in_b = w_in[0].astype(BF16)
    cos, sin = _rope_tables(s)
    g1 = norm1_g[0][None, :]
    q, k, v = _inproj(x, mod_x, g1, w_in_b, cos, sin, ts=ts, has_q=True)
    kc, vc = _inproj(ctx, mod_c, g1, w_in_b[:, d:], cos[:ctx.shape[1]], sin[:ctx.shape[1]],
                     ts=ctx.shape[1], has_q=False)

    lamv = jnp.stack([lam_q1[0], lam_k1[0], lam_q2[0], lam_k2[0]])
    o_d = _diff_attention(q, k, v, kc, vc, lamv, diff_subln_g[0][None, :], tq=tq)
    o_n = _natten(q, k, v, kc, vc, _na_bias_tables(na_rpb[0]))

    return _tail(x, o_d, o_n, mod_x, w_out[0].astype(BF16), norm2_g[0][None, :],
                 w_fc1[0].astype(BF16), w_fc2[0].astype(BF16), final_g[None, :], ts=ts)
```

```python
import functools
import math

import numpy as np
import jax
import jax.numpy as jnp
from jax import lax
from jax.experimental import pallas as pl
from jax.experimental.pallas import tpu as pltpu

GRID_W = 64
HEAD_DIM = 64
DIFF_HEADS = 4
NA_HEADS = 8
NA_KH = 8
NA_KW = 16
ROPE_BASE = 10000.0
ROPE_FREQS = HEAD_DIM // 4
EPS = 1e-6
NEG_INF = -1e30
LAM_INIT = 0.8 - 0.6 * math.exp(-0.3 * 0)

LANES = 128
SUBLANES = 8
VMEM_LIMIT = 56 * 1024 * 1024

NA_QROWS = 4
NA_KROWS = 12

BF16 = jnp.bfloat16
F32 = jnp.float32


def _params(*sem):
    return pltpu.CompilerParams(dimension_semantics=sem, vmem_limit_bytes=VMEM_LIMIT)


def _const_spec(shape):
    nd = len(shape)
    return pl.BlockSpec(shape, lambda *_: (0,) * nd, pipeline_mode=pl.Buffered(1))


def _dot(a, b):
    return jnp.dot(a, b, preferred_element_type=F32)


def _dot_nt(a, b):
    return lax.dot_general(a, b, (((1,), (1,)), ((), ())), preferred_element_type=F32)


def _rms(xf):
    return xf * lax.rsqrt(jnp.mean(xf * xf, axis=-1, keepdims=True) + EPS)


def _mod_kernel(cond_ref, w_ref, b_ref, o_ref):
    cnd = cond_ref[...]
    act = (cnd * jax.nn.sigmoid(cnd)).astype(BF16)
    o_ref[...] = _dot(act, w_ref[...].astype(BF16)) + b_ref[...]


def _mod(cond, w_mod, b_mod, tn=768):
    r, d = cond.shape
    n = w_mod.shape[1]
    return pl.pallas_call(
        _mod_kernel,
        out_shape=jax.ShapeDtypeStruct((r, n), F32),
        grid=(n // tn,),
        in_specs=[pl.BlockSpec((r, d), lambda j: (0, 0)),
                  pl.BlockSpec((d, tn), lambda j: (0, j)),
                  pl.BlockSpec((1, tn), lambda j: (0, j))],
        out_specs=pl.BlockSpec((r, tn), lambda j: (0, j)),
        compiler_params=_params("arbitrary"),
        name="mod",
    )(cond, w_mod, b_mod)


def _rope(xf, cos, sin_signed):
    lane = lax.broadcasted_iota(jnp.int32, xf.shape, 1)
    first = (lane % (2 * ROPE_FREQS)) < ROPE_FREQS
    partner = jnp.where(first,
                        pltpu.roll(xf, LANES - ROPE_FREQS, 1),
                        pltpu.roll(xf, ROPE_FREQS, 1))
    return xf * cos + partner * sin_signed


def _inproj_kernel(x_ref, mod_ref, g_ref, w_ref, wt_ref, cos_ref, sin_ref, *out_refs,
                   n_rope, q_scale, has_q, chunk):
    xf = x_ref[0]
    shift = mod_ref[0, 0:1, :]
    scale = mod_ref[0, 1:2, :]
    h = (_rms(xf) * g_ref[...]) * (1.0 + scale) + shift
    hb = h.astype(BF16)
    nat_refs, vt_ref = out_refs[:-1], out_refs[-1]
    col0 = 0
    for oi, o_ref in enumerate(nat_refs):
        width = o_ref.shape[-1]
        is_q = has_q and oi == 0
        roped = has_q and oi < 2
        for c0 in range(0, width, chunk):
            p = _dot(hb, w_ref[:, col0 + c0:col0 + c0 + chunk])
            for c in range(c0 // LANES, (c0 + chunk) // LANES):
                pc = p[:, c * LANES - c0:(c + 1) * LANES - c0]
                if roped and c < n_rope:
                    pc = _rope(pc, cos_ref[...], sin_ref[...])
                if is_q:
                    pc = pc * q_scale
                o_ref[0, :, c * LANES:(c + 1) * LANES] = pc.astype(o_ref.dtype)
        col0 += width
    vt_ref[0] = _dot_nt(wt_ref[...], hb).astype(vt_ref.dtype)


def _inproj(x, mod, g, w, wt, cos, sin, *, ts, has_q):
    b, t, d = x.shape
    half = wt.shape[0]
    nat_widths = ([d] if has_q else []) + [d, w.shape[1] - (2 * d if has_q else d)]
    per_batch_mod = mod.shape[0] > 1
    kern = functools.partial(_inproj_kernel, n_rope=(DIFF_HEADS if has_q else 0),
                             q_scale=HEAD_DIM ** -0.5, has_q=has_q, chunk=512)
    return pl.pallas_call(
        kern,
        out_shape=[jax.ShapeDtypeStruct((b, t, wd), BF16) for wd in nat_widths]
        + [jax.ShapeDtypeStruct((b, half, t), BF16)],
        grid=(b, t // ts),
        in_specs=[pl.BlockSpec((1, ts, d), lambda i, j: (i, j, 0)),
                  pl.BlockSpec((1,) + mod.shape[1:],
                               (lambda i, j: (i, 0, 0)) if per_batch_mod else (lambda i, j: (0, 0, 0))),
                  _const_spec(g.shape),
                  _const_spec(w.shape),
                  _const_spec(wt.shape),
                  pl.BlockSpec((ts, LANES), lambda i, j: (j, 0)),
                  pl.BlockSpec((ts, LANES), lambda i, j: (j, 0))],
        out_specs=[pl.BlockSpec((1, ts, wd), lambda i, j: (i, j, 0)) for wd in nat_widths]
        + [pl.BlockSpec((1, half, ts), lambda i, j: (i, 0, j))],
        compiler_params=_params("parallel", "arbitrary"),
        name="inproj_q" if has_q else "inproj_ctx",
    )(x, mod, g, w, wt, cos, sin)


def _diff_kernel(q_ref, k_ref, v_ref, kc_ref, vc_ref, lamv_ref, g_ref, o_ref):
    q = q_ref[0]
    lane = lax.broadcasted_iota(jnp.int32, q.shape, 1)
    lv = lamv_ref[...]
    lam = (jnp.exp(jnp.sum(lv[0:1] * lv[1:2], axis=-1, keepdims=True))
           - jnp.exp(jnp.sum(lv[2:3] * lv[3:4], axis=-1, keepdims=True)) + LAM_INIT)
    outs = []
    for i in range(2):
        sel = (lane < HEAD_DIM) if i == 0 else (lane >= HEAD_DIM)
        qi = jnp.where(sel, q, jnp.zeros_like(q))
        s_lat = _dot_nt(qi, k_ref[0])
        s_ctx = _dot_nt(qi, kc_ref[0])
        m = jnp.maximum(jnp.max(s_lat, axis=-1, keepdims=True),
                        jnp.max(s_ctx, axis=-1, keepdims=True))
        e_lat = jnp.exp(s_lat - m)
        e_ctx = jnp.exp(s_ctx - m)
        l = jnp.sum(e_lat, axis=-1, keepdims=True) + jnp.sum(e_ctx, axis=-1, keepdims=True)
        o = _dot(e_lat.astype(BF16), v_ref[0]) + _dot(e_ctx.astype(BF16), vc_ref[0])
        outs.append(o / l)
    o = outs[0] - lam * outs[1]
    o_ref[0] = (_rms(o) * g_ref[...] * (1.0 - LAM_INIT)).astype(o_ref.dtype)


def _diff_attention(q, k, v, kc, vc, lamv, g, *, tq):
    b, s, _ = q.shape
    c = kc.shape[1]
    w = 2 * HEAD_DIM
    return pl.pallas_call(
        _diff_kernel,
        out_shape=jax.ShapeDtypeStruct((b, s, DIFF_HEADS * w), BF16),
        grid=(b, DIFF_HEADS, s // tq),
        in_specs=[pl.BlockSpec((1, tq, w), lambda i, h, j: (i, j, h)),
                  pl.BlockSpec((1, s, w), lambda i, h, j: (i, 0, h)),
                  pl.BlockSpec((1, s, w), lambda i, h, j: (i, 0, h)),
                  pl.BlockSpec((1, c, w), lambda i, h, j: (i, 0, h)),
                  pl.BlockSpec((1, c, w), lambda i, h, j: (i, 0, h)),
                  pl.BlockSpec(lamv.shape, lambda i, h, j: (0, 0)),
                  pl.BlockSpec(g.shape, lambda i, h, j: (0, 0))],
        out_specs=pl.BlockSpec((1, tq, w), lambda i, h, j: (i, j, h)),
        compiler_params=_params("parallel", "arbitrary", "arbitrary"),
        name="diff_attn",
    )(q, k, v, kc, vc, lamv, g)


def _na_key_start(rb, rows):
    return jnp.clip(rb * NA_QROWS - NA_KH // 2, 0, rows - NA_KROWS)


def _natten_kernel(case_ref, q_ref, k_ref, vt_ref, kc_ref, vct_ref, bias_ref, o_ref, *, rows):
    del case_ref
    rb = pl.program_id(1)
    k0 = pl.multiple_of(_na_key_start(rb, rows) * GRID_W, NA_QROWS * GRID_W)
    n_lat = NA_KROWS * GRID_W
    n_q = NA_QROWS * GRID_W
    lane = lax.broadcasted_iota(jnp.int32, (n_q, LANES), 1)
    for hp in range(NA_HEADS // 2):
        cols = slice(hp * LANES, (hp + 1) * LANES)
        qp = q_ref[0, :, cols]
        kw = k_ref[0, pl.ds(k0, n_lat), cols]
        kcp = kc_ref[0, :, cols]
        halves = []
        for half in range(2):
            sel = (lane < HEAD_DIM) if half == 0 else (lane >= HEAD_DIM)
            qh = jnp.where(sel, qp, jnp.zeros_like(qp))
            st_lat = _dot_nt(kw, qh) + bias_ref[0, 2 * hp + half]
            st_ctx = _dot_nt(kcp, qh)
            m = jnp.maximum(jnp.max(st_lat, axis=0, keepdims=True),
                            jnp.max(st_ctx, axis=0, keepdims=True))
            e_lat = jnp.exp(st_lat - m)
            e_ctx = jnp.exp(st_ctx - m)
            l = jnp.sum(e_lat, axis=0, keepdims=True) + jnp.sum(e_ctx, axis=0, keepdims=True)
            ch = slice(hp * LANES + half * HEAD_DIM, hp * LANES + (half + 1) * HEAD_DIM)
            ot = (_dot(vt_ref[0, ch, pl.ds(k0, n_lat)], e_lat.astype(BF16))
                  + _dot(vct_ref[0, ch, :], e_ctx.astype(BF16)))
            halves.append(ot / l)
        o_ref[0, :, cols] = jnp.concatenate(halves, axis=0).T.astype(o_ref.dtype)


def _na_block_patterns(rows):
    pats = []
    for rb in range(rows // NA_QROWS):
        ustart = int(np.clip(rb * NA_QROWS - NA_KH // 2, 0, rows - NA_KROWS))
        pat = []
        for x in range(NA_KROWS):
            for rr in range(NA_QROWS):
                r = rb * NA_QROWS + rr
                start = int(np.clip(r - NA_KH // 2, 0, rows - NA_KH))
                a = ustart + x
                pat.append(a - r + NA_KH - 1 if start <= a < start + NA_KH else -1)
        pats.append(tuple(pat))
    return pats


def _na_bias_tables(rpb, rows):
    pats = _na_block_patterns(rows)
    cases = sorted(set(pats), key=pats.index)
    case_ids = np.array([cases.index(p) for p in pats], np.int32)

    jq = np.arange(GRID_W)
    col_start = np.clip(jq - NA_KW // 2, 0, GRID_W - NA_KW)
    in_win = (jq[None, :] >= col_start[:, None]) & (jq[None, :] < col_start[:, None] + NA_KW)
    dc_idx = np.clip(jq[None, :] - jq[:, None] + NA_KW - 1, 0, 2 * NA_KW - 2)
    onehot = (dc_idx.T[None] == np.arange(2 * NA_KW - 1)[:, None, None]) & in_win.T[None]
    toep = jnp.einsum("hdj,jkq->hdkq", rpb, jnp.asarray(onehot, F32),
                      precision=lax.Precision.HIGHEST)
    toep = jnp.where(jnp.asarray(in_win.T)[None, None], toep, NEG_INF)
    masked = jnp.full(toep.shape[:1] + toep.shape[2:], NEG_INF, F32)
    tables = []
    for pat in cases:
        blocks = [masked if dr < 0 else toep[:, dr] for dr in pat]
        t = jnp.stack(blocks, axis=1).reshape(NA_HEADS, NA_KROWS, NA_QROWS, GRID_W, GRID_W)
        t = jnp.transpose(t, (0, 1, 3, 2, 4))
        tables.append(t.reshape(NA_HEADS, NA_KROWS * GRID_W, NA_QROWS * GRID_W))
    return jnp.stack(tables), jnp.asarray(case_ids)


def _natten(q, k, vt, kc, vct, bias, case_ids):
    b, s, d = q.shape
    c = kc.shape[1]
    rows = s // GRID_W
    w = NA_HEADS * HEAD_DIM
    n_q = NA_QROWS * GRID_W
    grid_spec = pltpu.PrefetchScalarGridSpec(
        num_scalar_prefetch=1,
        grid=(b, rows // NA_QROWS),
        in_specs=[pl.BlockSpec((1, n_q, w), lambda i, r, cid: (i, r, 1)),
                  pl.BlockSpec((1, s, w), lambda i, r, cid: (i, 0, 1)),
                  pl.BlockSpec((1, w, s), lambda i, r, cid: (i, 0, 0)),
                  pl.BlockSpec((1, c, w), lambda i, r, cid: (i, 0, 1)),
                  pl.BlockSpec((1, w, c), lambda i, r, cid: (i, 0, 0)),
                  pl.BlockSpec((1,) + bias.shape[1:], lambda i, r, cid: (cid[r], 0, 0, 0))],
        out_specs=pl.BlockSpec((1, n_q, w), lambda i, r, cid: (i, r, 0)),
    )
    return pl.pallas_call(
        functools.partial(_natten_kernel, rows=rows),
        out_shape=jax.ShapeDtypeStruct((b, s, w), BF16),
        grid_spec=grid_spec,
        compiler_params=_params("parallel", "arbitrary"),
        name="natten",
    )(case_ids, q, k, vt, kc, vct, bias)


def _tail_kernel(x_ref, od_ref, on_ref, mod_ref, wo_ref, g2_ref, w1_ref, w2_ref, gf_ref, o_ref,
                 *, ff_chunk):
    half = od_ref.shape[-1]
    gate_a = mod_ref[0, 2:3, :]
    shift_m = mod_ref[0, 3:4, :]
    scale_m = mod_ref[0, 4:5, :]
    gate_m = mod_ref[0, 5:6, :]
    attn = _dot(od_ref[0], wo_ref[:half, :]) + _dot(on_ref[0], wo_ref[half:, :])
    x1 = x_ref[0] + gate_a * attn
    h2 = ((_rms(x1) * g2_ref[...]) * (1.0 + scale_m) + shift_m).astype(BF16)
    y = None
    for c in range(w1_ref.shape[1] // ff_chunk):
        cols = slice(c * ff_chunk, (c + 1) * ff_chunk)
        a = jnp.maximum(_dot(h2, w1_ref[:, cols]), 0.0)
        part = _dot((a * a).astype(BF16), w2_ref[cols, :])
        y = part if y is None else y + part
    x2 = x1 + gate_m * y
    o_ref[0] = _rms(x2) * gf_ref[...]


def _tail(x, o_d, o_n, mod, w_out, g2, w1, w2, gf, *, ts, ff_chunk=1024):
    b, s, d = x.shape
    half = o_d.shape[-1]
    return pl.pallas_call(
        functools.partial(_tail_kernel, ff_chunk=ff_chunk),
        out_shape=jax.ShapeDtypeStruct((b, s, d), F32),
        grid=(b, s // ts),
        in_specs=[pl.BlockSpec((1, ts, d), lambda i, j: (i, j, 0)),
                  pl.BlockSpec((1, ts, half), lambda i, j: (i, j, 0)),
                  pl.BlockSpec((1, ts, half), lambda i, j: (i, j, 0)),
                  pl.BlockSpec((1,) + mod.shape[1:], lambda i, j: (i, 0, 0)),
                  _const_spec(w_out.shape),
                  _const_spec(g2.shape),
                  _const_spec(w1.shape),
                  _const_spec(w2.shape),
                  _const_spec(gf.shape)],
        out_specs=pl.BlockSpec((1, ts, d), lambda i, j: (i, j, 0)),
        compiler_params=_params("parallel", "arbitrary"),
        name="tail",
    )(x, o_d, o_n, mod, w_out, g2, w1, w2, gf)


def _rope_tables(s):
    pos = np.arange(s)
    inv = ROPE_BASE ** (-np.arange(ROPE_FREQS, dtype=np.float32) / ROPE_FREQS)
    ang_r = (pos // GRID_W).astype(np.float32)[:, None] * inv
    ang_c = (pos % GRID_W).astype(np.float32)[:, None] * inv
    cos = np.concatenate([np.cos(ang_r)] * 2 + [np.cos(ang_c)] * 2, axis=1)
    sin = np.concatenate([-np.sin(ang_r), np.sin(ang_r), -np.sin(ang_c), np.sin(ang_c)], axis=1)
    reps = LANES // HEAD_DIM
    return (jnp.asarray(np.tile(cos, (1, reps)), F32), jnp.asarray(np.tile(sin, (1, reps)), F32))


def _tiles(s):
    ts = min(512, s)
    tq = min(512, s)
    return ts, tq


def kernel(x, c, ctx, c_ctx, w_mod, b_mod, norm1_g, w_in, lam_q1, lam_k1, lam_q2, lam_k2,
           diff_subln_g, na_rpb, w_out, norm2_g, w_fc1, w_fc2, final_g):
    b, s, d = x.shape
    rows = s // GRID_W
    assert w_mod.shape[0] == 1, "single-layer block"
    assert s % (GRID_W * NA_QROWS) == 0 and rows >= NA_KROWS
    ts, tq = _tiles(s)

    n_cond = b + 1
    pad = (-n_cond) % SUBLANES
    cond = jnp.concatenate([c, c_ctx[None, :], jnp.zeros((pad, d), F32)], axis=0)
    mod = _mod(cond, w_mod[0], b_mod[0][None, :])
    mod_x = mod[:b].reshape(b, 6, d)
    mod_c = mod[b:b + 1].reshape(1, 6, d)

    w_in_b = w_in[0].astype(BF16)
    n_nat = 2 * d + d // 2
    w_nat, w_vna_t = w_in_b[:, :n_nat], w_in_b[:, n_nat:].T
    cos, sin = _rope_tables(s)
    g1 = norm1_g[0][None, :]
    q, k, v_d, vt_n = _inproj(x, mod_x, g1, w_nat, w_vna_t, cos, sin, ts=ts, has_q=True)
    n_ctx = ctx.shape[1]
    kc, vc_d, vct_n = _inproj(ctx, mod_c, g1, w_nat[:, d:], w_vna_t, cos[:n_ctx], sin[:n_ctx],
                              ts=n_ctx, has_q=False)

    lamv = jnp.stack([lam_q1[0], lam_k1[0], lam_q2[0], lam_k2[0]])
    o_d = _diff_attention(q, k, v_d, kc, vc_d, lamv, diff_subln_g[0][None, :], tq=tq)
    bias, case_ids = _na_bias_tables(na_rpb[0], rows)
    o_n = _natten(q, k, vt_n, kc, vct_n, bias, case_ids)

    return _tail(x, o_d, o_n, mod_x, w_out[0].astype(BF16), norm2_g[0][None, :],
                 w_fc1[0].astype(BF16), w_fc2[0].astype(BF16), final_g[None, :], ts=ts)
```

```python
import functools
import math

import numpy as np
import jax
import jax.numpy as jnp
from jax import lax
from jax.experimental import pallas as pl
from jax.experimental.pallas import tpu as pltpu

GRID_W = 64
HEAD_DIM = 64
DIFF_HEADS = 4
NA_HEADS = 8
NA_KH = 8
NA_KW = 16
ROPE_BASE = 10000.0
ROPE_FREQS = HEAD_DIM // 4
EPS = 1e-6
NEG_INF = -1e30
LAM_INIT = 0.8 - 0.6 * math.exp(-0.3 * 0)
LOG2E = math.log2(math.e)

LANES = 128
SUBLANES = 8
VMEM_LIMIT = 56 * 1024 * 1024

NA_QROWS = 4
NA_KROWS = 12

BF16 = jnp.bfloat16
F32 = jnp.float32


def _params(*sem):
    return pltpu.CompilerParams(dimension_semantics=sem, vmem_limit_bytes=VMEM_LIMIT)


def _const_spec(shape):
    nd = len(shape)
    return pl.BlockSpec(shape, lambda *_: (0,) * nd, pipeline_mode=pl.Buffered(1))


def _dot(a, b):
    return jnp.dot(a, b, preferred_element_type=F32)


def _dot_nt(a, b):
    return lax.dot_general(a, b, (((1,), (1,)), ((), ())), preferred_element_type=F32)


def _rms(xf):
    return xf * lax.rsqrt(jnp.mean(xf * xf, axis=-1, keepdims=True) + EPS)


def _mod_kernel(cond_ref, w_ref, b_ref, o_ref):
    cnd = cond_ref[...]
    act = (cnd * jax.nn.sigmoid(cnd)).astype(BF16)
    o_ref[...] = _dot(act, w_ref[...].astype(BF16)) + b_ref[...]


def _mod(cond, w_mod, b_mod, tn=768):
    r, d = cond.shape
    n = w_mod.shape[1]
    return pl.pallas_call(
        _mod_kernel,
        out_shape=jax.ShapeDtypeStruct((r, n), F32),
        grid=(n // tn,),
        in_specs=[pl.BlockSpec((r, d), lambda j: (0, 0)),
                  pl.BlockSpec((d, tn), lambda j: (0, j)),
                  pl.BlockSpec((1, tn), lambda j: (0, j))],
        out_specs=pl.BlockSpec((r, tn), lambda j: (0, j)),
        compiler_params=_params("arbitrary"),
        name="mod",
    )(cond, w_mod, b_mod)


def _rope(xf, cos, sin_signed):
    lane = lax.broadcasted_iota(jnp.int32, xf.shape, 1)
    first = (lane % (2 * ROPE_FREQS)) < ROPE_FREQS
    partner = jnp.where(first,
                        pltpu.roll(xf, LANES - ROPE_FREQS, 1),
                        pltpu.roll(xf, ROPE_FREQS, 1))
    return xf * cos + partner * sin_signed


def _inproj_kernel(x_ref, mod_ref, g_ref, w_ref, wt_ref, cos_ref, sin_ref, *out_refs,
                   n_rope, q_scale, has_q, chunk):
    xf = x_ref[0]
    shift = mod_ref[0, 0:1, :]
    scale = mod_ref[0, 1:2, :]
    h = (_rms(xf) * g_ref[...]) * (1.0 + scale) + shift
    hb = h.astype(BF16)
    nat_refs, vt_ref = out_refs[:-1], out_refs[-1]
    col0 = 0
    for oi, o_ref in enumerate(nat_refs):
        width = o_ref.shape[-1]
        is_q = has_q and oi == 0
        roped = has_q and oi < 2
        for c0 in range(0, width, chunk):
            p = _dot(hb, w_ref[:, col0 + c0:col0 + c0 + chunk])
            for c in range(c0 // LANES, (c0 + chunk) // LANES):
                pc = p[:, c * LANES - c0:(c + 1) * LANES - c0]
                if roped and c < n_rope:
                    pc = _rope(pc, cos_ref[...], sin_ref[...])
                if is_q:
                    pc = pc * q_scale
                o_ref[0, :, c * LANES:(c + 1) * LANES] = pc.astype(o_ref.dtype)
        col0 += width
    vt_ref[0] = _dot_nt(wt_ref[...], hb).astype(vt_ref.dtype)


def _inproj(x, mod, g, w, wt, cos, sin, *, ts, has_q):
    b, t, d = x.shape
    half = wt.shape[0]
    nat_widths = [d] * (w.shape[1] // d)
    per_batch_mod = mod.shape[0] > 1
    kern = functools.partial(_inproj_kernel, n_rope=(DIFF_HEADS if has_q else 0),
                             q_scale=HEAD_DIM ** -0.5 * LOG2E, has_q=has_q, chunk=512)
    return pl.pallas_call(
        kern,
        out_shape=[jax.ShapeDtypeStruct((b, t, wd), BF16) for wd in nat_widths]
        + [jax.ShapeDtypeStruct((b, half, t), BF16)],
        grid=(b, t // ts),
        in_specs=[pl.BlockSpec((1, ts, d), lambda i, j: (i, j, 0)),
                  pl.BlockSpec((1,) + mod.shape[1:],
                               (lambda i, j: (i, 0, 0)) if per_batch_mod else (lambda i, j: (0, 0, 0))),
                  _const_spec(g.shape),
                  _const_spec(w.shape),
                  _const_spec(wt.shape),
                  pl.BlockSpec((ts, LANES), lambda i, j: (j, 0)),
                  pl.BlockSpec((ts, LANES), lambda i, j: (j, 0))],
        out_specs=[pl.BlockSpec((1, ts, wd), lambda i, j: (i, j, 0)) for wd in nat_widths]
        + [pl.BlockSpec((1, half, ts), lambda i, j: (i, 0, j))],
        compiler_params=_params("parallel", "arbitrary"),
        name="inproj_q" if has_q else "inproj_ctx",
    )(x, mod, g, w, wt, cos, sin)


def _ds(start, size):
    if isinstance(start, int):
        return pl.ds(start, size)
    return pl.ds(pl.multiple_of(start, size), size)


def _run_pipeline(n, stage1, stage2, stage3, bufs, *, unrolled):
    nb = len(bufs)

    def step(c, p, first=False, last=False):
        if not last:
            stage1(c + 1, bufs[(p + 1) % nb])
        stage2(c, bufs[p])
        if not first:
            stage3(c - 1, bufs[(p - 1) % nb])

    stage1(0, bufs[0])
    step(0, 0, first=True)
    step(1, 1)
    if unrolled:
        for c in range(2, n - 2):
            step(c, c % nb)
    else:
        assert nb == 2

        def body(cc, carry):
            step(2 * cc, 0)
            step(2 * cc + 1, 1)
            return carry
        lax.fori_loop(1, n // 2 - 1, body, 0)
    step(n - 2, (n - 2) % nb)
    step(n - 1, (n - 1) % nb, last=True)
    stage3(n - 1, bufs[(n - 1) % nb])


def _unrolled_pipeline(n, stages):
    for t in range(n + len(stages) - 1):
        for i, stage in enumerate(stages):
            if 0 <= t - i < n:
                stage(t - i)


def _diff_kernel(q_ref, k_ref, vt_ref, kc_ref, vct_ref, lamv_ref, g_ref, o_ref,
                 kall, vtall, st_a, m_a, ot_a, st_b, m_b, ot_b, *, tq):
    s = k_ref.shape[1]
    w = 2 * HEAD_DIM
    n_j = s // tq
    for h in range(DIFF_HEADS):
        cols = slice(h * w, (h + 1) * w)
        kall[h, 0:s, :] = k_ref[0, :, cols]
        kall[h, s:, :] = kc_ref[0, :, cols]
        vtall[h, 0:w, 0:s] = vt_ref[0, cols, :]
        vtall[h, 0:w, s:] = vct_ref[0, cols, :]
        vtall[h, w:, :] = jnp.ones((vtall.shape[1] - w, vtall.shape[2]), vtall.dtype)
    lv = lamv_ref[...]
    lam = (jnp.exp(jnp.sum(lv[0:1] * lv[1:2], axis=-1, keepdims=True))
           - jnp.exp(jnp.sum(lv[2:3] * lv[3:4], axis=-1, keepdims=True)) + LAM_INIT)

    def chunk(c):
        h, j = c // n_j, c % n_j
        return h, (_ds(j * tq, tq), _ds(h * w, w))

    def stage1(c, buf):
        st, mb, _ = buf
        h, idx = chunk(c)
        q = q_ref[(0,) + idx]
        lane = lax.broadcasted_iota(jnp.int32, q.shape, 1)
        for i in range(2):
            sel = (lane < HEAD_DIM) if i == 0 else (lane >= HEAD_DIM)
            qi = jnp.where(sel, q, jnp.zeros_like(q))
            x = _dot_nt(kall[h], qi)
            st[i] = x
            mb[i] = jnp.max(x, axis=0, keepdims=True)

    def stage2(c, buf):
        st, mb, ot = buf
        h, _ = chunk(c)
        for i in range(2):
            e = jnp.exp2(st[i] - mb[i]).astype(BF16)
            ot[i] = _dot(vtall[h], e)

    def stage3(c, buf):
        _, _, ot = buf
        _, idx = chunk(c)
        o1 = ot[0, 0:w, :] / ot[0, w:w + 1, :]
        o2 = ot[1, 0:w, :] / ot[1, w:w + 1, :]
        o = (o1 - lam * o2).T
        o_ref[(0,) + idx] = (_rms(o) * g_ref[...] * (1.0 - LAM_INIT)).astype(o_ref.dtype)

    _run_pipeline(DIFF_HEADS * n_j, stage1, stage2, stage3,
                  ((st_a, m_a, ot_a), (st_b, m_b, ot_b)), unrolled=False)


def _diff_attention(q, k, vt, kc, vct, lamv, g, *, tq):
    b, s, d = q.shape
    c = kc.shape[1]
    w = 2 * HEAD_DIM
    half = DIFF_HEADS * w
    ones_rows = 2 * SUBLANES
    hand_off = [pltpu.VMEM((2, s + c, tq), F32), pltpu.VMEM((2, 1, tq), F32),
                pltpu.VMEM((2, w + ones_rows, tq), F32)]
    return pl.pallas_call(
        functools.partial(_diff_kernel, tq=tq),
        out_shape=jax.ShapeDtypeStruct((b, s, half), BF16),
        grid=(b,),
        in_specs=[pl.BlockSpec((1, s, half), lambda i: (i, 0, 0)),
                  pl.BlockSpec((1, s, half), lambda i: (i, 0, 0)),
                  pl.BlockSpec((1, half, s), lambda i: (i, 0, 0)),
                  pl.BlockSpec((1, c, half), lambda i: (i, 0, 0)),
                  pl.BlockSpec((1, half, c), lambda i: (i, 0, 0)),
                  pl.BlockSpec(lamv.shape, lambda i: (0, 0)),
                  pl.BlockSpec(g.shape, lambda i: (0, 0))],
        out_specs=pl.BlockSpec((1, s, half), lambda i: (i, 0, 0)),
        scratch_shapes=[pltpu.VMEM((DIFF_HEADS, s + c, w), BF16),
                        pltpu.VMEM((DIFF_HEADS, w + ones_rows, s + c), BF16)] + hand_off + hand_off,
        compiler_params=_params("parallel"),
        name="diff_attn",
    )(q, k, vt, kc, vct, lamv, g)


def _na_key_start(rb, rows):
    return jnp.clip(rb * NA_QROWS - NA_KH // 2, 0, rows - NA_KROWS)


def _natten_kernel(case_ref, q_ref, k_ref, vt_ref, kc_ref, vct_ref, bias_ref, o_ref,
                   kall, vtall, on_buf, *hand_off, rows):
    del case_ref
    bufs = tuple(tuple(hand_off[i:i + 4]) for i in range(0, len(hand_off), 4))
    rb = pl.program_id(1)
    k0 = pl.multiple_of(_na_key_start(rb, rows) * GRID_W, NA_QROWS * GRID_W)
    n_lat = NA_KROWS * GRID_W
    for hp in range(NA_HEADS // 2):
        cols = slice(hp * LANES, (hp + 1) * LANES)
        kall[hp, 0:n_lat, :] = k_ref[0, pl.ds(k0, n_lat), cols]
        kall[hp, n_lat:, :] = kc_ref[0, :, cols]
    for h in range(NA_HEADS):
        ch = slice(h * HEAD_DIM, (h + 1) * HEAD_DIM)
        vtall[h, 0:HEAD_DIM, 0:n_lat] = vt_ref[0, ch, pl.ds(k0, n_lat)]
        vtall[h, 0:HEAD_DIM, n_lat:] = vct_ref[0, ch, :]
        vtall[h, HEAD_DIM:, :] = jnp.ones((vtall.shape[1] - HEAD_DIM, vtall.shape[2]), vtall.dtype)

    def scores(h):
        st, _, _, _ = bufs[h]
        hp, half = divmod(h, 2)
        qp = q_ref[0, :, hp * LANES:(hp + 1) * LANES]
        lane = lax.broadcasted_iota(jnp.int32, qp.shape, 1)
        sel = (lane < HEAD_DIM) if half == 0 else (lane >= HEAD_DIM)
        qh = jnp.where(sel, qp, jnp.zeros_like(qp))
        x = _dot_nt(kall[hp], qh)
        st[0:n_lat, :] = x[0:n_lat] + bias_ref[0, h]
        st[n_lat:, :] = x[n_lat:]

    def col_max(h):
        st, mb, _, _ = bufs[h]
        mb[...] = jnp.max(st[...], axis=0, keepdims=True)

    def exps(h):
        st, mb, eb, _ = bufs[h]
        eb[...] = jnp.exp2(st[...] - mb[...]).astype(BF16)

    def values(h):
        _, _, eb, ot = bufs[h]
        ot[...] = _dot(vtall[h], eb[...])

    def finish(h):
        _, _, _, ot = bufs[h]
        hp, half = divmod(h, 2)
        on_buf[hp, half * HEAD_DIM:(half + 1) * HEAD_DIM, :] = (
            ot[0:HEAD_DIM, :] / ot[HEAD_DIM:HEAD_DIM + 1, :])
        if half == 1:
            o_ref[0, :, hp * LANES:(hp + 1) * LANES] = on_buf[hp].T.astype(o_ref.dtype)

    _unrolled_pipeline(NA_HEADS, (scores, col_max, exps, values, finish))


def _na_block_patterns(rows):
    pats = []
    for rb in range(rows // NA_QROWS):
        ustart = int(np.clip(rb * NA_QROWS - NA_KH // 2, 0, rows - NA_KROWS))
        pat = []
        for x in range(NA_KROWS):
            for rr in range(NA_QROWS):
                r = rb * NA_QROWS + rr
                start = int(np.clip(r - NA_KH // 2, 0, rows - NA_KH))
                a = ustart + x
                pat.append(a - r + NA_KH - 1 if start <= a < start + NA_KH else -1)
        pats.append(tuple(pat))
    return pats


def _na_bias_tables(rpb, rows):
    pats = _na_block_patterns(rows)
    cases = sorted(set(pats), key=pats.index)
    case_ids = np.array([cases.index(p) for p in pats], np.int32)

    jq = np.arange(GRID_W)
    col_start = np.clip(jq - NA_KW // 2, 0, GRID_W - NA_KW)
    in_win = (jq[None, :] >= col_start[:, None]) & (jq[None, :] < col_start[:, None] + NA_KW)
    dc_idx = np.clip(jq[None, :] - jq[:, None] + NA_KW - 1, 0, 2 * NA_KW - 2)
    onehot = (dc_idx.T[None] == np.arange(2 * NA_KW - 1)[:, None, None]) & in_win.T[None]
    toep = jnp.einsum("hdj,jkq->hdkq", rpb, jnp.asarray(onehot, F32),
                      precision=lax.Precision.HIGHEST)
    toep = jnp.where(jnp.asarray(in_win.T)[None, None], toep * LOG2E, NEG_INF)
    masked = jnp.full(toep.shape[:1] + toep.shape[2:], NEG_INF, F32)
    tables = []
    for pat in cases:
        blocks = [masked if dr < 0 else toep[:, dr] for dr in pat]
        t = jnp.stack(blocks, axis=1).reshape(NA_HEADS, NA_KROWS, NA_QROWS, GRID_W, GRID_W)
        t = jnp.transpose(t, (0, 1, 3, 2, 4))
        tables.append(t.reshape(NA_HEADS, NA_KROWS * GRID_W, NA_QROWS * GRID_W))
    return jnp.stack(tables), jnp.asarray(case_ids)


def _natten(q, k, vt, kc, vct, bias, case_ids):
    b, s, d = q.shape
    c = kc.shape[1]
    rows = s // GRID_W
    w = NA_HEADS * HEAD_DIM
    n_q = NA_QROWS * GRID_W
    n_keys = NA_KROWS * GRID_W + c
    ones_rows = 2 * SUBLANES
    hand_off = [pltpu.VMEM((n_keys, n_q), F32), pltpu.VMEM((1, n_q), F32),
                pltpu.VMEM((n_keys, n_q), BF16), pltpu.VMEM((HEAD_DIM + ones_rows, n_q), F32)]
    grid_spec = pltpu.PrefetchScalarGridSpec(
        num_scalar_prefetch=1,
        grid=(b, rows // NA_QROWS),
        in_specs=[pl.BlockSpec((1, n_q, w), lambda i, r, cid: (i, r, 1)),
                  pl.BlockSpec((1, s, w), lambda i, r, cid: (i, 0, 1)),
                  pl.BlockSpec((1, w, s), lambda i, r, cid: (i, 1, 0)),
                  pl.BlockSpec((1, c, w), lambda i, r, cid: (i, 0, 1)),
                  pl.BlockSpec((1, w, c), lambda i, r, cid: (i, 1, 0)),
                  pl.BlockSpec((1,) + bias.shape[1:], lambda i, r, cid: (cid[r], 0, 0, 0))],
        out_specs=pl.BlockSpec((1, n_q, w), lambda i, r, cid: (i, r, 0)),
        scratch_shapes=[pltpu.VMEM((NA_HEADS // 2, n_keys, LANES), BF16),
                        pltpu.VMEM((NA_HEADS, HEAD_DIM + ones_rows, n_keys), BF16),
                        pltpu.VMEM((NA_HEADS // 2, LANES, n_q), F32)] + hand_off * NA_HEADS,
    )
    return pl.pallas_call(
        functools.partial(_natten_kernel, rows=rows),
        out_shape=jax.ShapeDtypeStruct((b, s, w), BF16),
        grid_spec=grid_spec,
        compiler_params=_params("parallel", "arbitrary"),
        name="natten",
    )(case_ids, q, k, vt, kc, vct, bias)


def _tail_kernel(x_ref, od_ref, on_ref, mod_ref, wo_ref, g2_ref, w1_ref, w2_ref, gf_ref, o_ref,
                 *, ff_chunk):
    half = od_ref.shape[-1]
    gate_a = mod_ref[0, 2:3, :]
    shift_m = mod_ref[0, 3:4, :]
    scale_m = mod_ref[0, 4:5, :]
    gate_m = mod_ref[0, 5:6, :]
    attn = _dot(od_ref[0], wo_ref[:half, :]) + _dot(on_ref[0], wo_ref[half:, :])
    x1 = x_ref[0] + gate_a * attn
    h2 = ((_rms(x1) * g2_ref[...]) * (1.0 + scale_m) + shift_m).astype(BF16)
    y = None
    for c in range(w1_ref.shape[1] // ff_chunk):
        cols = slice(c * ff_chunk, (c + 1) * ff_chunk)
        a = jnp.maximum(_dot(h2, w1_ref[:, cols]), 0.0)
        part = _dot((a * a).astype(BF16), w2_ref[cols, :])
        y = part if y is None else y + part
    x2 = x1 + gate_m * y
    o_ref[0] = _rms(x2) * gf_ref[...]


def _tail(x, o_d, o_n, mod, w_out, g2, w1, w2, gf, *, ts, ff_chunk=1024):
    b, s, d = x.shape
    half = o_d.shape[-1]
    return pl.pallas_call(
        functools.partial(_tail_kernel, ff_chunk=ff_chunk),
        out_shape=jax.ShapeDtypeStruct((b, s, d), F32),
        grid=(b, s // ts),
        in_specs=[pl.BlockSpec((1, ts, d), lambda i, j: (i, j, 0)),
                  pl.BlockSpec((1, ts, half), lambda i, j: (i, j, 0)),
                  pl.BlockSpec((1, ts, half), lambda i, j: (i, j, 0)),
                  pl.BlockSpec((1,) + mod.shape[1:], lambda i, j: (i, 0, 0)),
                  _const_spec(w_out.shape),
                  _const_spec(g2.shape),
                  _const_spec(w1.shape),
                  _const_spec(w2.shape),
                  _const_spec(gf.shape)],
        out_specs=pl.BlockSpec((1, ts, d), lambda i, j: (i, j, 0)),
        compiler_params=_params("parallel", "arbitrary"),
        name="tail",
    )(x, o_d, o_n, mod, w_out, g2, w1, w2, gf)


def _rope_tables(s):
    pos = np.arange(s)
    inv = ROPE_BASE ** (-np.arange(ROPE_FREQS, dtype=np.float32) / ROPE_FREQS)
    ang_r = (pos // GRID_W).astype(np.float32)[:, None] * inv
    ang_c = (pos % GRID_W).astype(np.float32)[:, None] * inv
    cos = np.concatenate([np.cos(ang_r)] * 2 + [np.cos(ang_c)] * 2, axis=1)
    sin = np.concatenate([-np.sin(ang_r), np.sin(ang_r), -np.sin(ang_c), np.sin(ang_c)], axis=1)
    reps = LANES // HEAD_DIM
    return (jnp.asarray(np.tile(cos, (1, reps)), F32), jnp.asarray(np.tile(sin, (1, reps)), F32))


def _tiles(s):
    ts = min(512, s)
    tq = min(256, s)
    return ts, tq


def kernel(x, c, ctx, c_ctx, w_mod, b_mod, norm1_g, w_in, lam_q1, lam_k1, lam_q2, lam_k2,
           diff_subln_g, na_rpb, w_out, norm2_g, w_fc1, w_fc2, final_g):
    b, s, d = x.shape
    rows = s // GRID_W
    assert w_mod.shape[0] == 1, "single-layer block"
    assert s % (GRID_W * NA_QROWS) == 0 and rows >= NA_KROWS
    ts, tq = _tiles(s)

    n_cond = b + 1
    pad = (-n_cond) % SUBLANES
    cond = jnp.concatenate([c, c_ctx[None, :], jnp.zeros((pad, d), F32)], axis=0)
    mod = _mod(cond, w_mod[0], b_mod[0][None, :])
    mod_x = mod[:b].reshape(b, 6, d)
    mod_c = mod[b:b + 1].reshape(1, 6, d)

    w_in_b = w_in[0].astype(BF16)
    w_qk, w_v_t = w_in_b[:, :2 * d], w_in_b[:, 2 * d:].T
    cos, sin = _rope_tables(s)
    g1 = norm1_g[0][None, :]
    q, k, vt = _inproj(x, mod_x, g1, w_qk, w_v_t, cos, sin, ts=ts, has_q=True)
    n_ctx = ctx.shape[1]
    kc, vct = _inproj(ctx, mod_c, g1, w_qk[:, d:], w_v_t, cos[:n_ctx], sin[:n_ctx],
                      ts=n_ctx, has_q=False)

    lamv = jnp.stack([lam_q1[0], lam_k1[0], lam_q2[0], lam_k2[0]])
    o_d = _diff_attention(q, k, vt, kc, vct, lamv, diff_subln_g[0][None, :], tq=tq)
    bias, case_ids = _na_bias_tables(na_rpb[0], rows)
    o_n = _natten(q, k, vt, kc, vct, bias, case_ids)

    return _tail(x, o_d, o_n, mod_x, w_out[0].astype(BF16), norm2_g[0][None, :],
                 w_fc1[0].astype(BF16), w_fc2[0].astype(BF16), final_g[None, :], ts=ts)
```

```python
import functools
import math

import numpy as np
import jax
import jax.numpy as jnp
from jax import lax
from jax.experimental import pallas as pl
from jax.experimental.pallas import tpu as pltpu

GRID_W = 64
HEAD_DIM = 64
DIFF_HEADS = 4
NA_HEADS = 8
NA_KH = 8
NA_KW = 16
ROPE_BASE = 10000.0
ROPE_FREQS = HEAD_DIM // 4
EPS = 1e-6
NEG_INF = -1e30
LAM_INIT = 0.8 - 0.6 * math.exp(-0.3 * 0)
LOG2E = math.log2(math.e)

LANES = 128
SUBLANES = 8
VMEM_LIMIT = 56 * 1024 * 1024

NA_QROWS = 4
NA_KROWS = 12
NA_SLAB = HEAD_DIM + 16
NA_SETS = 4
DIFF_ORDER = (0, 1, 2)
NA_ORDER = (3, 0, 1, 2, 4)

BF16 = jnp.bfloat16
F32 = jnp.float32


def _params(*sem):
    return pltpu.CompilerParams(dimension_semantics=sem, vmem_limit_bytes=VMEM_LIMIT)


def _const_spec(shape):
    nd = len(shape)
    return pl.BlockSpec(shape, lambda *_: (0,) * nd, pipeline_mode=pl.Buffered(1))


def _dot(a, b):
    return jnp.dot(a, b, preferred_element_type=F32)


def _dot_nt(a, b):
    return lax.dot_general(a, b, (((1,), (1,)), ((), ())), preferred_element_type=F32)


def _rms(xf):
    return xf * lax.rsqrt(jnp.mean(xf * xf, axis=-1, keepdims=True) + EPS)


def _mod_kernel(cond_ref, w_ref, b_ref, o_ref):
    cnd = cond_ref[...]
    act = (cnd * jax.nn.sigmoid(cnd)).astype(BF16)
    o_ref[...] = _dot(act, w_ref[...].astype(BF16)) + b_ref[...]


def _mod(cond, w_mod, b_mod, tn=768):
    r, d = cond.shape
    n = w_mod.shape[1]
    return pl.pallas_call(
        _mod_kernel,
        out_shape=jax.ShapeDtypeStruct((r, n), F32),
        grid=(n // tn,),
        in_specs=[pl.BlockSpec((r, d), lambda j: (0, 0)),
                  pl.BlockSpec((d, tn), lambda j: (0, j)),
                  pl.BlockSpec((1, tn), lambda j: (0, j))],
        out_specs=pl.BlockSpec((r, tn), lambda j: (0, j)),
        compiler_params=_params("arbitrary"),
        name="mod",
    )(cond, w_mod, b_mod)


def _rope(xf, cos, sin_signed):
    lane = lax.broadcasted_iota(jnp.int32, xf.shape, 1)
    first = (lane % (2 * ROPE_FREQS)) < ROPE_FREQS
    partner = jnp.where(first,
                        pltpu.roll(xf, LANES - ROPE_FREQS, 1),
                        pltpu.roll(xf, ROPE_FREQS, 1))
    return xf * cos + partner * sin_signed


def _inproj_kernel(x_ref, mod_ref, g_ref, w_ref, wt_ref, cos_ref, sin_ref, *out_refs,
                   n_rope, q_scale, has_q, chunk):
    xf = x_ref[0]
    shift = mod_ref[0, 0:1, :]
    scale = mod_ref[0, 1:2, :]
    h = (_rms(xf) * g_ref[...]) * (1.0 + scale) + shift
    hb = h.astype(BF16)
    nat_refs, vtd_ref, vtn_ref = out_refs[:-2], out_refs[-2], out_refs[-1]
    col0 = 0
    for oi, o_ref in enumerate(nat_refs):
        width = o_ref.shape[-1]
        is_q = has_q and oi == 0
        roped = has_q and oi < 2
        for c0 in range(0, width, chunk):
            p = _dot(hb, w_ref[:, col0 + c0:col0 + c0 + chunk])
            for c in range(c0 // LANES, (c0 + chunk) // LANES):
                pc = p[:, c * LANES - c0:(c + 1) * LANES - c0]
                if roped and c < n_rope:
                    pc = _rope(pc, cos_ref[...], sin_ref[...])
                if is_q:
                    pc = pc * q_scale
                o_ref[0, :, c * LANES:(c + 1) * LANES] = pc.astype(o_ref.dtype)
        col0 += width
    n_d = vtd_ref.shape[1]
    vtd_ref[0] = _dot_nt(wt_ref[0:n_d, :], hb).astype(vtd_ref.dtype)
    vn = _dot_nt(wt_ref[n_d:, :], hb)
    for hd in range(NA_HEADS):
        r0 = hd * NA_SLAB
        vtn_ref[0, r0:r0 + HEAD_DIM, :] = vn[hd * HEAD_DIM:(hd + 1) * HEAD_DIM].astype(vtn_ref.dtype)
        vtn_ref[0, r0 + HEAD_DIM:r0 + NA_SLAB, :] = jnp.ones((NA_SLAB - HEAD_DIM, vn.shape[1]),
                                                             vtn_ref.dtype)


def _inproj(x, mod, g, w, wt, cos, sin, *, ts, has_q):
    b, t, d = x.shape
    t_widths = [DIFF_HEADS * 2 * HEAD_DIM, NA_HEADS * NA_SLAB]
    nat_widths = [d] * (w.shape[1] // d)
    per_batch_mod = mod.shape[0] > 1
    kern = functools.partial(_inproj_kernel, n_rope=(DIFF_HEADS if has_q else 0),
                             q_scale=HEAD_DIM ** -0.5 * LOG2E, has_q=has_q, chunk=512)
    return pl.pallas_call(
        kern,
        out_shape=[jax.ShapeDtypeStruct((b, t, wd), BF16) for wd in nat_widths]
        + [jax.ShapeDtypeStruct((b, wd, t), BF16) for wd in t_widths],
        grid=(b, t // ts),
        in_specs=[pl.BlockSpec((1, ts, d), lambda i, j: (i, j, 0)),
                  pl.BlockSpec((1,) + mod.shape[1:],
                               (lambda i, j: (i, 0, 0)) if per_batch_mod else (lambda i, j: (0, 0, 0))),
                  _const_spec(g.shape),
                  _const_spec(w.shape),
                  _const_spec(wt.shape),
                  pl.BlockSpec((ts, LANES), lambda i, j: (j, 0)),
                  pl.BlockSpec((ts, LANES), lambda i, j: (j, 0))],
        out_specs=[pl.BlockSpec((1, ts, wd), lambda i, j: (i, j, 0)) for wd in nat_widths]
        + [pl.BlockSpec((1, wd, ts), lambda i, j: (i, 0, j)) for wd in t_widths],
        compiler_params=_params("parallel", "arbitrary"),
        name="inproj_q" if has_q else "inproj_ctx",
    )(x, mod, g, w, wt, cos, sin)


def _ds(start, size):
    if isinstance(start, int):
        return pl.ds(start, size)
    return pl.ds(pl.multiple_of(start, size), size)


def _run_pipeline(n, stages, n_sets, issue_order=None):
    k = len(stages)
    issue_order = tuple(range(k)) if issue_order is None else issue_order
    assert sorted(issue_order) == list(range(k))
    assert n % n_sets == 0 and n >= n_sets + k

    def step(t, u):
        for i in issue_order:
            c = t - i
            if isinstance(c, int) and not 0 <= c < n:
                continue
            stages[i](c, (u - i) % n_sets)

    t0 = -(-(k - 1) // n_sets) * n_sets
    for t in range(t0):
        step(t, t % n_sets)

    def body(tt, carry):
        for u in range(n_sets):
            step(tt * n_sets + u, u)
        return carry

    lax.fori_loop(t0 // n_sets, n // n_sets, body, 0)
    for t in range(n, n + k - 1):
        step(t, t % n_sets)


def _diff_kernel(q_ref, k_ref, vt_ref, kc_ref, vct_ref, lamv_ref, g_ref, o_ref,
                 kall, vtall, *hand_off, tq):
    sets = tuple(tuple(hand_off[i:i + 3]) for i in range(0, len(hand_off), 3))
    s = k_ref.shape[1]
    w = 2 * HEAD_DIM
    n_j = s // tq
    for h in range(DIFF_HEADS):
        cols = slice(h * w, (h + 1) * w)
        kall[h, 0:s, :] = k_ref[0, :, cols]
        kall[h, s:, :] = kc_ref[0, :, cols]
        vtall[h, 0:w, 0:s] = vt_ref[0, cols, :]
        vtall[h, 0:w, s:] = vct_ref[0, cols, :]
        vtall[h, w:, :] = jnp.ones((vtall.shape[1] - w, vtall.shape[2]), vtall.dtype)
    lv = lamv_ref[...]
    lam = (jnp.exp(jnp.sum(lv[0:1] * lv[1:2], axis=-1, keepdims=True))
           - jnp.exp(jnp.sum(lv[2:3] * lv[3:4], axis=-1, keepdims=True)) + LAM_INIT)

    def chunk(c):
        h, j = c // n_j, c % n_j
        return h, (_ds(j * tq, tq), _ds(h * w, w))

    def scores(c, p):
        st, mb, _ = sets[p]
        h, idx = chunk(c)
        q = q_ref[(0,) + idx]
        lane = lax.broadcasted_iota(jnp.int32, q.shape, 1)
        for i in range(2):
            sel = (lane < HEAD_DIM) if i == 0 else (lane >= HEAD_DIM)
            qi = jnp.where(sel, q, jnp.zeros_like(q))
            x = _dot_nt(kall[h], qi)
            st[i] = x
            mb[i] = jnp.max(x, axis=0, keepdims=True)

    def values(c, p):
        st, mb, ot = sets[p]
        h, _ = chunk(c)
        for i in range(2):
            e = jnp.exp2(st[i] - mb[i]).astype(BF16)
            ot[i] = _dot(vtall[h], e)

    def finish(c, p):
        _, _, ot = sets[p]
        _, idx = chunk(c)
        o1 = ot[0, 0:w, :] / ot[0, w:w + 1, :]
        o2 = ot[1, 0:w, :] / ot[1, w:w + 1, :]
        o = (o1 - lam * o2).T
        o_ref[(0,) + idx] = (_rms(o) * g_ref[...] * (1.0 - LAM_INIT)).astype(o_ref.dtype)

    _run_pipeline(DIFF_HEADS * n_j, (scores, values, finish), len(sets), DIFF_ORDER)


def _diff_attention(q, k, vt, kc, vct, lamv, g, *, tq):
    b, s, d = q.shape
    c = kc.shape[1]
    w = 2 * HEAD_DIM
    half = DIFF_HEADS * w
    ones_rows = 2 * SUBLANES
    hand_off = [pltpu.VMEM((2, s + c, tq), F32), pltpu.VMEM((2, 1, tq), F32),
                pltpu.VMEM((2, w + ones_rows, tq), F32)]
    return pl.pallas_call(
        functools.partial(_diff_kernel, tq=tq),
        out_shape=jax.ShapeDtypeStruct((b, s, half), BF16),
        grid=(b,),
        in_specs=[pl.BlockSpec((1, s, half), lambda i: (i, 0, 0)),
                  pl.BlockSpec((1, s, half), lambda i: (i, 0, 0)),
                  pl.BlockSpec((1, half, s), lambda i: (i, 0, 0)),
                  pl.BlockSpec((1, c, half), lambda i: (i, 0, 0)),
                  pl.BlockSpec((1, half, c), lambda i: (i, 0, 0)),
                  pl.BlockSpec(lamv.shape, lambda i: (0, 0)),
                  pl.BlockSpec(g.shape, lambda i: (0, 0))],
        out_specs=pl.BlockSpec((1, s, half), lambda i: (i, 0, 0)),
        scratch_shapes=[pltpu.VMEM((DIFF_HEADS, s + c, w), BF16),
                        pltpu.VMEM((DIFF_HEADS, w + ones_rows, s + c), BF16)] + hand_off * 2,
        compiler_params=_params("parallel"),
        name="diff_attn",
    )(q, k, vt, kc, vct, lamv, g)


def _natten_kernel(case_ref, q_ref, k_ref, vt_ref, kc_ref, vct_ref, bias_ref, o_ref,
                   on_buf, *hand_off, rows):
    sets = tuple(tuple(hand_off[i:i + 4]) for i in range(0, len(hand_off), 4))
    n_lat = NA_KROWS * GRID_W
    n_q = NA_QROWS * GRID_W

    def item(c):
        rb, h = c // NA_HEADS, c % NA_HEADS
        ustart = rb * NA_QROWS - NA_KH // 2
        if isinstance(c, int):
            k_rows = pl.ds(min(max(ustart, 0), rows - NA_KROWS) * GRID_W, n_lat)
            slab = pl.ds(h * NA_SLAB, NA_SLAB)
        else:
            k_rows = pl.ds(pl.multiple_of(jnp.clip(ustart, 0, rows - NA_KROWS) * GRID_W, n_q), n_lat)
            slab = pl.ds(pl.multiple_of(h * NA_SLAB, 2 * SUBLANES), NA_SLAB)
        return rb, h, k_rows, _ds(rb * n_q, n_q), _ds((h // 2) * LANES, LANES), slab

    def scores(c, p):
        st = sets[p][0]
        rb, h, k_rows, q_rows, pair_cols, _ = item(c)
        qp = q_ref[0, q_rows, pair_cols]
        lane = lax.broadcasted_iota(jnp.int32, qp.shape, 1)
        sel = (lane < HEAD_DIM) if p % 2 == 0 else (lane >= HEAD_DIM)
        qh = jnp.where(sel, qp, jnp.zeros_like(qp))
        x_lat = _dot_nt(k_ref[0, k_rows, pair_cols], qh)
        st[0:n_lat, :] = x_lat + bias_ref[case_ref[rb], h]
        st[n_lat:, :] = _dot_nt(kc_ref[0, :, pair_cols], qh)

    def col_max(c, p):
        st, mb, _, _ = sets[p]
        mb[...] = jnp.max(st[...], axis=0, keepdims=True)

    def exps(c, p):
        st, mb, eb, _ = sets[p]
        eb[...] = jnp.exp2(st[...] - mb[...]).astype(BF16)

    def values(c, p):
        _, _, eb, ot = sets[p]
        _, _, k_rows, _, _, slab = item(c)
        ot[...] = (_dot(vt_ref[0, slab, k_rows], eb[0:n_lat, :])
                   + _dot(vct_ref[0, slab, :], eb[n_lat:, :]))

    def finish(c, p):
        _, _, _, ot = sets[p]
        _, _, _, q_rows, pair_cols, _ = item(c)
        half = p % 2
        on_buf[half * HEAD_DIM:(half + 1) * HEAD_DIM, :] = (
            ot[0:HEAD_DIM, :] / ot[HEAD_DIM:HEAD_DIM + 1, :])
        if half == 1:
            o_ref[0, q_rows, pair_cols] = on_buf[...].T.astype(o_ref.dtype)

    assert len(sets) % 2 == 0
    _run_pipeline((rows // NA_QROWS) * NA_HEADS, (scores, col_max, exps, values, finish), len(sets),
                  NA_ORDER)


def _na_block_patterns(rows):
    pats = []
    for rb in range(rows // NA_QROWS):
        ustart = int(np.clip(rb * NA_QROWS - NA_KH // 2, 0, rows - NA_KROWS))
        pat = []
        for x in range(NA_KROWS):
            for rr in range(NA_QROWS):
                r = rb * NA_QROWS + rr
                start = int(np.clip(r - NA_KH // 2, 0, rows - NA_KH))
                a = ustart + x
                pat.append(a - r + NA_KH - 1 if start <= a < start + NA_KH else -1)
        pats.append(tuple(pat))
    return pats


def _na_bias_tables(rpb, rows):
    pats = _na_block_patterns(rows)
    cases = sorted(set(pats), key=pats.index)
    case_ids = np.array([cases.index(p) for p in pats], np.int32)

    jq = np.arange(GRID_W)
    col_start = np.clip(jq - NA_KW // 2, 0, GRID_W - NA_KW)
    in_win = (jq[None, :] >= col_start[:, None]) & (jq[None, :] < col_start[:, None] + NA_KW)
    dc_idx = np.clip(jq[None, :] - jq[:, None] + NA_KW - 1, 0, 2 * NA_KW - 2)
    onehot = (dc_idx.T[None] == np.arange(2 * NA_KW - 1)[:, None, None]) & in_win.T[None]
    pat = np.array(cases).reshape(len(cases), NA_KROWS, NA_QROWS)
    block_sel = (pat[..., None] == np.arange(2 * NA_KH - 1)).astype(np.float32)
    valid = (pat >= 0)[:, None, :, None, :, None] & in_win.T[None, None, None, :, None, :]
    exact = lax.Precision.HIGHEST
    toep = jnp.einsum("hdj,jkq->hdkq", rpb, jnp.asarray(onehot, F32), precision=exact)
    t = jnp.einsum("cxrd,hdkq->chxkrq", jnp.asarray(block_sel), toep, precision=exact)
    t = jnp.where(jnp.asarray(valid), t * LOG2E, NEG_INF)
    t = t.reshape(len(cases), NA_HEADS, NA_KROWS * GRID_W, NA_QROWS * GRID_W)
    return t, jnp.asarray(case_ids)


def _natten(q, k, vt, kc, vct, bias, case_ids):
    b, s, d = q.shape
    c = kc.shape[1]
    rows = s // GRID_W
    w = NA_HEADS * HEAD_DIM
    n_q = NA_QROWS * GRID_W
    n_keys = NA_KROWS * GRID_W + c
    hand_off = [pltpu.VMEM((n_keys, n_q), F32), pltpu.VMEM((1, n_q), F32),
                pltpu.VMEM((n_keys, n_q), BF16),
                pltpu.VMEM((NA_SLAB, n_q), F32)]
    grid_spec = pltpu.PrefetchScalarGridSpec(
        num_scalar_prefetch=1,
        grid=(b,),
        in_specs=[pl.BlockSpec((1, s, w), lambda i, cid: (i, 0, 1)),
                  pl.BlockSpec((1, s, w), lambda i, cid: (i, 0, 1)),
                  pl.BlockSpec((1,) + vt.shape[1:], lambda i, cid: (i, 0, 0)),
                  pl.BlockSpec((1, c, w), lambda i, cid: (i, 0, 1)),
                  pl.BlockSpec((1,) + vct.shape[1:], lambda i, cid: (i, 0, 0)),
                  _const_spec(bias.shape)],
        out_specs=pl.BlockSpec((1, s, w), lambda i, cid: (i, 0, 0)),
        scratch_shapes=[pltpu.VMEM((LANES, n_q), F32)] + hand_off * NA_SETS,
    )
    return pl.pallas_call(
        functools.partial(_natten_kernel, rows=rows),
        out_shape=jax.ShapeDtypeStruct((b, s, w), BF16),
        grid_spec=grid_spec,
        compiler_params=_params("parallel"),
        name="natten",
    )(case_ids, q, k, vt, kc, vct, bias)


def _tail_kernel(x_ref, od_ref, on_ref, mod_ref, wo_ref, g2_ref, w1_ref, w2_ref, gf_ref, o_ref,
                 *, ff_chunk):
    half = od_ref.shape[-1]
    gate_a = mod_ref[0, 2:3, :]
    shift_m = mod_ref[0, 3:4, :]
    scale_m = mod_ref[0, 4:5, :]
    gate_m = mod_ref[0, 5:6, :]
    attn = _dot(od_ref[0], wo_ref[:half, :]) + _dot(on_ref[0], wo_ref[half:, :])
    x1 = x_ref[0] + gate_a * attn
    h2 = ((_rms(x1) * g2_ref[...]) * (1.0 + scale_m) + shift_m).astype(BF16)
    y = None
    for c in range(w1_ref.shape[1] // ff_chunk):
        cols = slice(c * ff_chunk, (c + 1) * ff_chunk)
        a = jnp.maximum(_dot(h2, w1_ref[:, cols]), 0.0)
        part = _dot((a * a).astype(BF16), w2_ref[cols, :])
        y = part if y is None else y + part
    x2 = x1 + gate_m * y
    o_ref[0] = _rms(x2) * gf_ref[...]


def _tail(x, o_d, o_n, mod, w_out, g2, w1, w2, gf, *, ts, ff_chunk=1024):
    b, s, d = x.shape
    half = o_d.shape[-1]
    return pl.pallas_call(
        functools.partial(_tail_kernel, ff_chunk=ff_chunk),
        out_shape=jax.ShapeDtypeStruct((b, s, d), F32),
        grid=(b, s // ts),
        in_specs=[pl.BlockSpec((1, ts, d), lambda i, j: (i, j, 0)),
                  pl.BlockSpec((1, ts, half), lambda i, j: (i, j, 0)),
                  pl.BlockSpec((1, ts, half), lambda i, j: (i, j, 0)),
                  pl.BlockSpec((1,) + mod.shape[1:], lambda i, j: (i, 0, 0)),
                  _const_spec(w_out.shape),
                  _const_spec(g2.shape),
                  _const_spec(w1.shape),
                  _const_spec(w2.shape),
                  _const_spec(gf.shape)],
        out_specs=pl.BlockSpec((1, ts, d), lambda i, j: (i, j, 0)),
        compiler_params=_params("parallel", "arbitrary"),
        name="tail",
    )(x, o_d, o_n, mod, w_out, g2, w1, w2, gf)


def _rope_tables(s):
    pos = np.arange(s)
    inv = ROPE_BASE ** (-np.arange(ROPE_FREQS, dtype=np.float32) / ROPE_FREQS)
    ang_r = (pos // GRID_W).astype(np.float32)[:, None] * inv
    ang_c = (pos % GRID_W).astype(np.float32)[:, None] * inv
    cos = np.concatenate([np.cos(ang_r)] * 2 + [np.cos(ang_c)] * 2, axis=1)
    sin = np.concatenate([-np.sin(ang_r), np.sin(ang_r), -np.sin(ang_c), np.sin(ang_c)], axis=1)
    reps = LANES // HEAD_DIM
    return (jnp.asarray(np.tile(cos, (1, reps)), F32), jnp.asarray(np.tile(sin, (1, reps)), F32))


def _tiles(s):
    ts = min(512, s)
    tq = min(256, s)
    return ts, tq


def kernel(x, c, ctx, c_ctx, w_mod, b_mod, norm1_g, w_in, lam_q1, lam_k1, lam_q2, lam_k2,
           diff_subln_g, na_rpb, w_out, norm2_g, w_fc1, w_fc2, final_g):
    b, s, d = x.shape
    rows = s // GRID_W
    assert w_mod.shape[0] == 1, "single-layer block"
    assert s % (GRID_W * NA_QROWS) == 0 and rows >= NA_KROWS
    ts, tq = _tiles(s)

    n_cond = b + 1
    pad = (-n_cond) % SUBLANES
    cond = jnp.concatenate([c, c_ctx[None, :], jnp.zeros((pad, d), F32)], axis=0)
    mod = _mod(cond, w_mod[0], b_mod[0][None, :])
    mod_x = mod[:b].reshape(b, 6, d)
    mod_c = mod[b:b + 1].reshape(1, 6, d)

    w_in_b = w_in[0].astype(BF16)
    w_qk, w_v_t = w_in_b[:, :2 * d], w_in_b[:, 2 * d:].T
    cos, sin = _rope_tables(s)
    g1 = norm1_g[0][None, :]
    q, k, vt_d, vt_n = _inproj(x, mod_x, g1, w_qk, w_v_t, cos, sin, ts=ts, has_q=True)
    n_ctx = ctx.shape[1]
    kc, vct_d, vct_n = _inproj(ctx, mod_c, g1, w_qk[:, d:], w_v_t, cos[:n_ctx], sin[:n_ctx],
                               ts=n_ctx, has_q=False)

    lamv = jnp.stack([lam_q1[0], lam_k1[0], lam_q2[0], lam_k2[0]])
    o_d = _diff_attention(q, k, vt_d, kc, vct_d, lamv, diff_subln_g[0][None, :], tq=tq)
    bias, case_ids = _na_bias_tables(na_rpb[0], rows)
    o_n = _natten(q, k, vt_n, kc, vct_n, bias, case_ids)

    return _tail(x, o_d, o_n, mod_x, w_out[0].astype(BF16), norm2_g[0][None, :],
                 w_fc1[0].astype(BF16), w_fc2[0].astype(BF16), final_g[None, :], ts=ts)
```

```python
import functools
import math

import numpy as np
import jax
import jax.numpy as jnp
from jax import lax
from jax.experimental import pallas as pl
from jax.experimental.pallas import tpu as pltpu

GRID_W = 64
HEAD_DIM = 64
DIFF_HEADS = 4
NA_HEADS = 8
NA_KH = 8
NA_KW = 16
ROPE_BASE = 10000.0
ROPE_FREQS = HEAD_DIM // 4
EPS = 1e-6
NEG_INF = -1e30
LAM_INIT = 0.8 - 0.6 * math.exp(-0.3 * 0)
LOG2E = math.log2(math.e)

LANES = 128
SUBLANES = 8
VMEM_LIMIT = 56 * 1024 * 1024

NA_QROWS = 4
NA_KROWS = 12
NA_SLAB = HEAD_DIM + 16
NA_SETS = 4
DIFF_ORDER = (0, 1, 2)
NA_ORDER = (3, 0, 1, 2, 4)

BF16 = jnp.bfloat16
F32 = jnp.float32


def _params(*sem):
    return pltpu.CompilerParams(dimension_semantics=sem, vmem_limit_bytes=VMEM_LIMIT)


def _const_spec(shape):
    nd = len(shape)
    return pl.BlockSpec(shape, lambda *_: (0,) * nd, pipeline_mode=pl.Buffered(1))


def _dot(a, b):
    return jnp.dot(a, b, preferred_element_type=F32)


def _dot_nt(a, b):
    return lax.dot_general(a, b, (((1,), (1,)), ((), ())), preferred_element_type=F32)


def _rms(xf):
    return xf * lax.rsqrt(jnp.mean(xf * xf, axis=-1, keepdims=True) + EPS)


def _mod_kernel(cond_ref, w_ref, b_ref, o_ref):
    cnd = cond_ref[...]
    act = (cnd * jax.nn.sigmoid(cnd)).astype(BF16)
    o_ref[...] = _dot(act, w_ref[...].astype(BF16)) + b_ref[...]


def _mod(cond, w_mod, b_mod, tn=768):
    r, d = cond.shape
    n = w_mod.shape[1]
    return pl.pallas_call(
        _mod_kernel,
        out_shape=jax.ShapeDtypeStruct((r, n), F32),
        grid=(n // tn,),
        in_specs=[pl.BlockSpec((r, d), lambda j: (0, 0)),
                  pl.BlockSpec((d, tn), lambda j: (0, j)),
                  pl.BlockSpec((1, tn), lambda j: (0, j))],
        out_specs=pl.BlockSpec((r, tn), lambda j: (0, j)),
        compiler_params=_params("arbitrary"),
        name="mod",
    )(cond, w_mod, b_mod)


def _rope(xf, cos, sin_signed):
    lane = lax.broadcasted_iota(jnp.int32, xf.shape, 1)
    first = (lane % (2 * ROPE_FREQS)) < ROPE_FREQS
    partner = jnp.where(first,
                        pltpu.roll(xf, LANES - ROPE_FREQS, 1),
                        pltpu.roll(xf, ROPE_FREQS, 1))
    return xf * cos + partner * sin_signed


def _inproj_kernel(x_ref, mod_ref, g_ref, w_ref, wt_ref, cos_ref, sin_ref, *out_refs,
                   n_rope, q_scale, has_q, chunk):
    xf = x_ref[0]
    shift = mod_ref[0, 0:1, :]
    scale = mod_ref[0, 1:2, :]
    h = (_rms(xf) * g_ref[...]) * (1.0 + scale) + shift
    hb = h.astype(BF16)
    nat_refs, vtd_ref, vtn_ref = out_refs[:-2], out_refs[-2], out_refs[-1]
    col0 = 0
    for oi, o_ref in enumerate(nat_refs):
        width = o_ref.shape[-1]
        is_q = has_q and oi == 0
        roped = has_q and oi < 2
        for c0 in range(0, width, chunk):
            p = _dot(hb, w_ref[:, col0 + c0:col0 + c0 + chunk])
            for c in range(c0 // LANES, (c0 + chunk) // LANES):
                pc = p[:, c * LANES - c0:(c + 1) * LANES - c0]
                if roped and c < n_rope:
                    pc = _rope(pc, cos_ref[...], sin_ref[...])
                if is_q:
                    pc = pc * q_scale
                o_ref[0, :, c * LANES:(c + 1) * LANES] = pc.astype(o_ref.dtype)
        col0 += width
    n_d = vtd_ref.shape[1]
    vtd_ref[0] = _dot_nt(wt_ref[0:n_d, :], hb).astype(vtd_ref.dtype)
    vn = _dot_nt(wt_ref[n_d:, :], hb)
    for hd in range(NA_HEADS):
        r0 = hd * NA_SLAB
        vtn_ref[0, r0:r0 + HEAD_DIM, :] = vn[hd * HEAD_DIM:(hd + 1) * HEAD_DIM].astype(vtn_ref.dtype)
        vtn_ref[0, r0 + HEAD_DIM:r0 + NA_SLAB, :] = jnp.ones((NA_SLAB - HEAD_DIM, vn.shape[1]),
                                                             vtn_ref.dtype)


def _inproj(x, mod, g, w, wt, cos, sin, *, ts, has_q):
    b, t, d = x.shape
    t_widths = [DIFF_HEADS * 2 * HEAD_DIM, NA_HEADS * NA_SLAB]
    nat_widths = [d] * (w.shape[1] // d)
    per_batch_mod = mod.shape[0] > 1
    kern = functools.partial(_inproj_kernel, n_rope=(DIFF_HEADS if has_q else 0),
                             q_scale=HEAD_DIM ** -0.5 * LOG2E, has_q=has_q, chunk=512)
    return pl.pallas_call(
        kern,
        out_shape=[jax.ShapeDtypeStruct((b, t, wd), BF16) for wd in nat_widths]
        + [jax.ShapeDtypeStruct((b, wd, t), BF16) for wd in t_widths],
        grid=(b, t // ts),
        in_specs=[pl.BlockSpec((1, ts, d), lambda i, j: (i, j, 0)),
                  pl.BlockSpec((1,) + mod.shape[1:],
                               (lambda i, j: (i, 0, 0)) if per_batch_mod else (lambda i, j: (0, 0, 0))),
                  _const_spec(g.shape),
                  _const_spec(w.shape),
                  _const_spec(wt.shape),
                  pl.BlockSpec((ts, LANES), lambda i, j: (j, 0)),
                  pl.BlockSpec((ts, LANES), lambda i, j: (j, 0))],
        out_specs=[pl.BlockSpec((1, ts, wd), lambda i, j: (i, j, 0)) for wd in nat_widths]
        + [pl.BlockSpec((1, wd, ts), lambda i, j: (i, 0, j)) for wd in t_widths],
        compiler_params=_params("parallel", "arbitrary"),
        name="inproj_q" if has_q else "inproj_ctx",
    )(x, mod, g, w, wt, cos, sin)


def _ds(start, size):
    if isinstance(start, int):
        return pl.ds(start, size)
    return pl.ds(pl.multiple_of(start, size), size)


def _run_pipeline(n, stages, n_sets, issue_order=None):
    k = len(stages)
    issue_order = tuple(range(k)) if issue_order is None else issue_order
    assert sorted(issue_order) == list(range(k))
    assert n % n_sets == 0 and n >= n_sets + k

    def step(t, u):
        for i in issue_order:
            c = t - i
            if isinstance(c, int) and not 0 <= c < n:
                continue
            stages[i](c, (u - i) % n_sets)

    t0 = -(-(k - 1) // n_sets) * n_sets
    for t in range(t0):
        step(t, t % n_sets)

    def body(tt, carry):
        for u in range(n_sets):
            step(tt * n_sets + u, u)
        return carry

    lax.fori_loop(t0 // n_sets, n // n_sets, body, 0)
    for t in range(n, n + k - 1):
        step(t, t % n_sets)


def _diff_kernel(q_ref, k_ref, vt_ref, kc_ref, vct_ref, lamv_ref, g_ref, o_ref,
                 kall, vtall, *hand_off, tq):
    sets = tuple(tuple(hand_off[i:i + 3]) for i in range(0, len(hand_off), 3))
    s = k_ref.shape[1]
    w = 2 * HEAD_DIM
    n_j = s // tq
    for h in range(DIFF_HEADS):
        cols = slice(h * w, (h + 1) * w)
        kall[h, 0:s, :] = k_ref[0, :, cols]
        kall[h, s:, :] = kc_ref[0, :, cols]
        vtall[h, 0:w, 0:s] = vt_ref[0, cols, :]
        vtall[h, 0:w, s:] = vct_ref[0, cols, :]
        vtall[h, w:, :] = jnp.ones((vtall.shape[1] - w, vtall.shape[2]), vtall.dtype)
    lv = lamv_ref[...]
    lam = (jnp.exp(jnp.sum(lv[0:1] * lv[1:2], axis=-1, keepdims=True))
           - jnp.exp(jnp.sum(lv[2:3] * lv[3:4], axis=-1, keepdims=True)) + LAM_INIT)

    def chunk(c):
        h, j = c // n_j, c % n_j
        return h, (_ds(j * tq, tq), _ds(h * w, w))

    def scores(c, p):
        st, mb, _ = sets[p]
        h, idx = chunk(c)
        q = q_ref[(0,) + idx]
        lane = lax.broadcasted_iota(jnp.int32, q.shape, 1)
        for i in range(2):
            sel = (lane < HEAD_DIM) if i == 0 else (lane >= HEAD_DIM)
            qi = jnp.where(sel, q, jnp.zeros_like(q))
            x = _dot_nt(kall[h], qi)
            st[i] = x
            mb[i] = jnp.max(x, axis=0, keepdims=True)

    def values(c, p):
        st, mb, ot = sets[p]
        h, _ = chunk(c)
        for i in range(2):
            e = jnp.exp2(st[i] - mb[i]).astype(BF16)
            ot[i] = _dot(vtall[h], e)

    def finish(c, p):
        _, _, ot = sets[p]
        _, idx = chunk(c)
        o1 = ot[0, 0:w, :] / ot[0, w:w + 1, :]
        o2 = ot[1, 0:w, :] / ot[1, w:w + 1, :]
        o = (o1 - lam * o2).T
        o_ref[(0,) + idx] = (_rms(o) * g_ref[...] * (1.0 - LAM_INIT)).astype(o_ref.dtype)

    _run_pipeline(DIFF_HEADS * n_j, (scores, values, finish), len(sets), DIFF_ORDER)


def _diff_attention(q, k, vt, kc, vct, lamv, g, *, tq):
    b, s, d = q.shape
    c = kc.shape[1]
    w = 2 * HEAD_DIM
    half = DIFF_HEADS * w
    ones_rows = 2 * SUBLANES
    hand_off = [pltpu.VMEM((2, s + c, tq), F32), pltpu.VMEM((2, 1, tq), F32),
                pltpu.VMEM((2, w + ones_rows, tq), F32)]
    return pl.pallas_call(
        functools.partial(_diff_kernel, tq=tq),
        out_shape=jax.ShapeDtypeStruct((b, s, half), BF16),
        grid=(b,),
        in_specs=[pl.BlockSpec((1, s, half), lambda i: (i, 0, 0)),
                  pl.BlockSpec((1, s, half), lambda i: (i, 0, 0)),
                  pl.BlockSpec((1, half, s), lambda i: (i, 0, 0)),
                  pl.BlockSpec((1, c, half), lambda i: (i, 0, 0)),
                  pl.BlockSpec((1, half, c), lambda i: (i, 0, 0)),
                  pl.BlockSpec(lamv.shape, lambda i: (0, 0)),
                  pl.BlockSpec(g.shape, lambda i: (0, 0))],
        out_specs=pl.BlockSpec((1, s, half), lambda i: (i, 0, 0)),
        scratch_shapes=[pltpu.VMEM((DIFF_HEADS, s + c, w), BF16),
                        pltpu.VMEM((DIFF_HEADS, w + ones_rows, s + c), BF16)] + hand_off * 2,
        compiler_params=_params("parallel"),
        name="diff_attn",
    )(q, k, vt, kc, vct, lamv, g)


def _natten_kernel(case_ref, q_ref, k_ref, vt_ref, kc_ref, vct_ref, toep_ref, o_ref,
                   bias_tab, on_buf, *hand_off, rows, cases):
    sets = tuple(tuple(hand_off[i:i + 4]) for i in range(0, len(hand_off), 4))
    n_lat = NA_KROWS * GRID_W
    n_q = NA_QROWS * GRID_W

    @pl.when(pl.program_id(0) == 0)
    def _():
        lane = lax.broadcasted_iota(jnp.int32, (GRID_W, LANES), 1)
        masked = jnp.full((GRID_W, LANES), NEG_INF, F32)
        for ci, pat in enumerate(cases):
            for x in range(NA_KROWS):
                for rp in range(NA_QROWS // 2):
                    d0, d1 = pat[x * NA_QROWS + 2 * rp], pat[x * NA_QROWS + 2 * rp + 1]
                    for h in range(NA_HEADS):
                        left = toep_ref[h, d0] if d0 >= 0 else masked
                        right = toep_ref[h, d1] if d1 >= 0 else masked
                        blk = masked if max(d0, d1) < 0 else jnp.where(lane < GRID_W, left, right)
                        bias_tab[ci, h, x * GRID_W:(x + 1) * GRID_W, rp * LANES:(rp + 1) * LANES] = blk

    def item(c):
        rb, h = c // NA_HEADS, c % NA_HEADS
        ustart = rb * NA_QROWS - NA_KH // 2
        if isinstance(c, int):
            k_rows = pl.ds(min(max(ustart, 0), rows - NA_KROWS) * GRID_W, n_lat)
            slab = pl.ds(h * NA_SLAB, NA_SLAB)
        else:
            k_rows = pl.ds(pl.multiple_of(jnp.clip(ustart, 0, rows - NA_KROWS) * GRID_W, n_q), n_lat)
            slab = pl.ds(pl.multiple_of(h * NA_SLAB, 2 * SUBLANES), NA_SLAB)
        return rb, h, k_rows, _ds(rb * n_q, n_q), _ds((h // 2) * LANES, LANES), slab

    def scores(c, p):
        st = sets[p][0]
        rb, h, k_rows, q_rows, pair_cols, _ = item(c)
        qp = q_ref[0, q_rows, pair_cols]
        lane = lax.broadcasted_iota(jnp.int32, qp.shape, 1)
        sel = (lane < HEAD_DIM) if p % 2 == 0 else (lane >= HEAD_DIM)
        qh = jnp.where(sel, qp, jnp.zeros_like(qp))
        x_lat = _dot_nt(k_ref[0, k_rows, pair_cols], qh)
        st[0:n_lat, :] = x_lat + bias_tab[case_ref[rb], h]
        st[n_lat:, :] = _dot_nt(kc_ref[0, :, pair_cols], qh)

    def col_max(c, p):
        st, mb, _, _ = sets[p]
        mb[...] = jnp.max(st[...], axis=0, keepdims=True)

    def exps(c, p):
        st, mb, eb, _ = sets[p]
        eb[...] = jnp.exp2((st[...] - mb[...]).astype(BF16))

    def values(c, p):
        _, _, eb, ot = sets[p]
        _, _, k_rows, _, _, slab = item(c)
        ot[...] = (_dot(vt_ref[0, slab, k_rows], eb[0:n_lat, :])
                   + _dot(vct_ref[0, slab, :], eb[n_lat:, :]))

    def finish(c, p):
        _, _, _, ot = sets[p]
        _, _, _, q_rows, pair_cols, _ = item(c)
        half = p % 2
        on_buf[half * HEAD_DIM:(half + 1) * HEAD_DIM, :] = (
            ot[0:HEAD_DIM, :] / ot[HEAD_DIM:HEAD_DIM + 1, :])
        if half == 1:
            o_ref[0, q_rows, pair_cols] = on_buf[...].T.astype(o_ref.dtype)

    assert len(sets) % 2 == 0
    _run_pipeline((rows // NA_QROWS) * NA_HEADS, (scores, col_max, exps, values, finish), len(sets),
                  NA_ORDER)


def _na_block_patterns(rows):
    pats = []
    for rb in range(rows // NA_QROWS):
        ustart = int(np.clip(rb * NA_QROWS - NA_KH // 2, 0, rows - NA_KROWS))
        pat = []
        for x in range(NA_KROWS):
            for rr in range(NA_QROWS):
                r = rb * NA_QROWS + rr
                start = int(np.clip(r - NA_KH // 2, 0, rows - NA_KH))
                a = ustart + x
                pat.append(a - r + NA_KH - 1 if start <= a < start + NA_KH else -1)
        pats.append(tuple(pat))
    return pats


def _na_block_cases(rows):
    pats = _na_block_patterns(rows)
    cases = tuple(sorted(set(pats), key=pats.index))
    return cases, np.array([cases.index(p) for p in pats], np.int32)


def _na_toeplitz(rpb):
    jq = np.arange(GRID_W)
    col_start = np.clip(jq - NA_KW // 2, 0, GRID_W - NA_KW)
    in_win = (jq[None, :] >= col_start[:, None]) & (jq[None, :] < col_start[:, None] + NA_KW)
    dc_idx = np.clip(jq[None, :] - jq[:, None] + NA_KW - 1, 0, 2 * NA_KW - 2)
    onehot = (dc_idx.T[None] == np.arange(2 * NA_KW - 1)[:, None, None]) & in_win.T[None]
    toep = jnp.einsum("hdj,jkq->hdkq", rpb, jnp.asarray(onehot, F32),
                      precision=lax.Precision.HIGHEST)
    toep = jnp.where(jnp.asarray(in_win.T)[None, None], toep * LOG2E, NEG_INF)
    return jnp.concatenate([toep, toep], axis=-1)


def _natten(q, k, vt, kc, vct, toep, cases, case_ids):
    b, s, d = q.shape
    c = kc.shape[1]
    rows = s // GRID_W
    w = NA_HEADS * HEAD_DIM
    n_q = NA_QROWS * GRID_W
    n_keys = NA_KROWS * GRID_W + c
    hand_off = [pltpu.VMEM((n_keys, n_q), F32), pltpu.VMEM((1, n_q), F32),
                pltpu.VMEM((n_keys, n_q), BF16),
                pltpu.VMEM((NA_SLAB, n_q), F32)]
    grid_spec = pltpu.PrefetchScalarGridSpec(
        num_scalar_prefetch=1,
        grid=(b,),
        in_specs=[pl.BlockSpec((1, s, w), lambda i, cid: (i, 0, 1)),
                  pl.BlockSpec((1, s, w), lambda i, cid: (i, 0, 1)),
                  pl.BlockSpec((1,) + vt.shape[1:], lambda i, cid: (i, 0, 0)),
                  pl.BlockSpec((1, c, w), lambda i, cid: (i, 0, 1)),
                  pl.BlockSpec((1,) + vct.shape[1:], lambda i, cid: (i, 0, 0)),
                  _const_spec(toep.shape)],
        out_specs=pl.BlockSpec((1, s, w), lambda i, cid: (i, 0, 0)),
        scratch_shapes=[pltpu.VMEM((len(cases), NA_HEADS, NA_KROWS * GRID_W, n_q), F32),
                        pltpu.VMEM((LANES, n_q), F32)] + hand_off * NA_SETS,
    )
    return pl.pallas_call(
        functools.partial(_natten_kernel, rows=rows, cases=cases),
        out_shape=jax.ShapeDtypeStruct((b, s, w), BF16),
        grid_spec=grid_spec,
        compiler_params=_params("arbitrary"),
        name="natten",
    )(case_ids, q, k, vt, kc, vct, toep)


def _tail_kernel(x_ref, od_ref, on_ref, mod_ref, wo_ref, g2_ref, w1_ref, w2_ref, gf_ref, o_ref,
                 *, ff_chunk):
    half = od_ref.shape[-1]
    gate_a = mod_ref[0, 2:3, :]
    shift_m = mod_ref[0, 3:4, :]
    scale_m = mod_ref[0, 4:5, :]
    gate_m = mod_ref[0, 5:6, :]
    attn = _dot(od_ref[0], wo_ref[:half, :]) + _dot(on_ref[0], wo_ref[half:, :])
    x1 = x_ref[0] + gate_a * attn
    h2 = ((_rms(x1) * g2_ref[...]) * (1.0 + scale_m) + shift_m).astype(BF16)
    y = None
    for c in range(w1_ref.shape[1] // ff_chunk):
        cols = slice(c * ff_chunk, (c + 1) * ff_chunk)
        a = jnp.maximum(_dot(h2, w1_ref[:, cols]), 0.0)
        part = _dot((a * a).astype(BF16), w2_ref[cols, :])
        y = part if y is None else y + part
    x2 = x1 + gate_m * y
    o_ref[0] = _rms(x2) * gf_ref[...]


def _tail(x, o_d, o_n, mod, w_out, g2, w1, w2, gf, *, ts, ff_chunk=1024):
    b, s, d = x.shape
    half = o_d.shape[-1]
    return pl.pallas_call(
        functools.partial(_tail_kernel, ff_chunk=ff_chunk),
        out_shape=jax.ShapeDtypeStruct((b, s, d), F32),
        grid=(b, s // ts),
        in_specs=[pl.BlockSpec((1, ts, d), lambda i, j: (i, j, 0)),
                  pl.BlockSpec((1, ts, half), lambda i, j: (i, j, 0)),
                  pl.BlockSpec((1, ts, half), lambda i, j: (i, j, 0)),
                  pl.BlockSpec((1,) + mod.shape[1:], lambda i, j: (i, 0, 0)),
                  _const_spec(w_out.shape),
                  _const_spec(g2.shape),
                  _const_spec(w1.shape),
                  _const_spec(w2.shape),
                  _const_spec(gf.shape)],
        out_specs=pl.BlockSpec((1, ts, d), lambda i, j: (i, j, 0)),
        compiler_params=_params("parallel", "arbitrary"),
        name="tail",
    )(x, o_d, o_n, mod, w_out, g2, w1, w2, gf)


def _rope_tables(s):
    pos = np.arange(s)
    inv = ROPE_BASE ** (-np.arange(ROPE_FREQS, dtype=np.float32) / ROPE_FREQS)
    ang_r = (pos // GRID_W).astype(np.float32)[:, None] * inv
    ang_c = (pos % GRID_W).astype(np.float32)[:, None] * inv
    cos = np.concatenate([np.cos(ang_r)] * 2 + [np.cos(ang_c)] * 2, axis=1)
    sin = np.concatenate([-np.sin(ang_r), np.sin(ang_r), -np.sin(ang_c), np.sin(ang_c)], axis=1)
    reps = LANES // HEAD_DIM
    return (jnp.asarray(np.tile(cos, (1, reps)), F32), jnp.asarray(np.tile(sin, (1, reps)), F32))


def _tiles(s):
    return min(1024, s), min(512, s), min(256, s)


def kernel(x, c, ctx, c_ctx, w_mod, b_mod, norm1_g, w_in, lam_q1, lam_k1, lam_q2, lam_k2,
           diff_subln_g, na_rpb, w_out, norm2_g, w_fc1, w_fc2, final_g):
    b, s, d = x.shape
    rows = s // GRID_W
    assert w_mod.shape[0] == 1, "single-layer block"
    assert s % (GRID_W * NA_QROWS) == 0 and rows >= NA_KROWS
    ts_proj, ts_tail, tq = _tiles(s)

    n_cond = b + 1
    pad = (-n_cond) % SUBLANES
    cond = jnp.concatenate([c, c_ctx[None, :], jnp.zeros((pad, d), F32)], axis=0)
    mod = _mod(cond, w_mod[0], b_mod[0][None, :])
    mod_x = mod[:b].reshape(b, 6, d)
    mod_c = mod[b:b + 1].reshape(1, 6, d)

    w_in_b = w_in[0].astype(BF16)
    w_qk, w_v_t = w_in_b[:, :2 * d], w_in_b[:, 2 * d:].T
    cos, sin = _rope_tables(s)
    g1 = norm1_g[0][None, :]
    q, k, vt_d, vt_n = _inproj(x, mod_x, g1, w_qk, w_v_t, cos, sin, ts=ts_proj, has_q=True)
    n_ctx = ctx.shape[1]
    kc, vct_d, vct_n = _inproj(ctx, mod_c, g1, w_qk[:, d:], w_v_t, cos[:n_ctx], sin[:n_ctx],
                               ts=n_ctx, has_q=False)

    lamv = jnp.stack([lam_q1[0], lam_k1[0], lam_q2[0], lam_k2[0]])
    o_d = _diff_attention(q, k, vt_d, kc, vct_d, lamv, diff_subln_g[0][None, :], tq=tq)
    cases, case_ids = _na_block_cases(rows)
    o_n = _natten(q, k, vt_n, kc, vct_n, _na_toeplitz(na_rpb[0]), cases, jnp.asarray(case_ids))

    return _tail(x, o_d, o_n, mod_x, w_out[0].astype(BF16), norm2_g[0][None, :],
                 w_fc1[0].astype(BF16), w_fc2[0].astype(BF16), final_g[None, :], ts=ts_tail)
```

```python
import functools
import math

import numpy as np
import jax
import jax.numpy as jnp
from jax import lax
from jax.experimental import pallas as pl
from jax.experimental.pallas import tpu as pltpu

GRID_W = 64
HEAD_DIM = 64
DIFF_HEADS = 4
NA_HEADS = 8
NA_KH = 8
NA_KW = 16
ROPE_BASE = 10000.0
ROPE_FREQS = HEAD_DIM // 4
EPS = 1e-6
NEG_INF = -1e30
LAM_INIT = 0.8 - 0.6 * math.exp(-0.3 * 0)
LOG2E = math.log2(math.e)

LANES = 128
SUBLANES = 8
VMEM_LIMIT = 56 * 1024 * 1024

NA_QROWS = 4
NA_KROWS = 12
NA_SLAB = HEAD_DIM + 16
NA_SETS = 4
DIFF_ORDER = (0, 1, 2)
NA_ORDER = (3, 0, 1, 2, 4)

BF16 = jnp.bfloat16
F32 = jnp.float32


def _params(*sem):
    return pltpu.CompilerParams(dimension_semantics=sem, vmem_limit_bytes=VMEM_LIMIT)


def _const_spec(shape):
    nd = len(shape)
    return pl.BlockSpec(shape, lambda *_: (0,) * nd, pipeline_mode=pl.Buffered(1))


def _dot(a, b):
    return jnp.dot(a, b, preferred_element_type=F32)


def _dot_nt(a, b):
    return lax.dot_general(a, b, (((1,), (1,)), ((), ())), preferred_element_type=F32)


def _rms(xf):
    return xf * lax.rsqrt(jnp.mean(xf * xf, axis=-1, keepdims=True) + EPS)


def _mod_kernel(cond_ref, w_ref, b_ref, o_ref):
    cnd = cond_ref[...]
    act = (cnd * jax.nn.sigmoid(cnd)).astype(BF16)
    o_ref[...] = _dot(act, w_ref[...].astype(BF16)) + b_ref[...]


def _mod(cond, w_mod, b_mod, tn=768):
    r, d = cond.shape
    n = w_mod.shape[1]
    return pl.pallas_call(
        _mod_kernel,
        out_shape=jax.ShapeDtypeStruct((r, n), F32),
        grid=(n // tn,),
        in_specs=[pl.BlockSpec((r, d), lambda j: (0, 0)),
                  pl.BlockSpec((d, tn), lambda j: (0, j)),
                  pl.BlockSpec((1, tn), lambda j: (0, j))],
        out_specs=pl.BlockSpec((r, tn), lambda j: (0, j)),
        compiler_params=_params("arbitrary"),
        name="mod",
    )(cond, w_mod, b_mod)


def _rope(xf, cos, sin_signed):
    lane = lax.broadcasted_iota(jnp.int32, xf.shape, 1)
    first = (lane % (2 * ROPE_FREQS)) < ROPE_FREQS
    partner = jnp.where(first,
                        pltpu.roll(xf, LANES - ROPE_FREQS, 1),
                        pltpu.roll(xf, ROPE_FREQS, 1))
    return xf * cos + partner * sin_signed


def _inproj_kernel(x_ref, mod_ref, g_ref, w_ref, wt_ref, cos_ref, sin_ref, *out_refs,
                   n_rope, q_scale, has_q, chunk):
    xf = x_ref[0]
    shift = mod_ref[0, 0:1, :]
    scale = mod_ref[0, 1:2, :]
    h = (_rms(xf) * g_ref[...]) * (1.0 + scale) + shift
    hb = h.astype(BF16)
    nat_refs, vtd_ref, vtn_ref = out_refs[:-2], out_refs[-2], out_refs[-1]
    col0 = 0
    for oi, o_ref in enumerate(nat_refs):
        width = o_ref.shape[-1]
        is_q = has_q and oi == 0
        roped = has_q and oi < 2
        for c0 in range(0, width, chunk):
            p = _dot(hb, w_ref[:, col0 + c0:col0 + c0 + chunk])
            for c in range(c0 // LANES, (c0 + chunk) // LANES):
                pc = p[:, c * LANES - c0:(c + 1) * LANES - c0]
                if roped and c < n_rope:
                    pc = _rope(pc, cos_ref[...], sin_ref[...])
                if is_q:
                    pc = pc * q_scale
                o_ref[0, :, c * LANES:(c + 1) * LANES] = pc.astype(o_ref.dtype)
        col0 += width
    n_d = vtd_ref.shape[1]
    vtd_ref[0] = _dot_nt(wt_ref[0:n_d, :], hb).astype(vtd_ref.dtype)
    vn = _dot_nt(wt_ref[n_d:, :], hb)
    for hd in range(NA_HEADS):
        r0 = hd * NA_SLAB
        vtn_ref[0, r0:r0 + HEAD_DIM, :] = vn[hd * HEAD_DIM:(hd + 1) * HEAD_DIM].astype(vtn_ref.dtype)
        vtn_ref[0, r0 + HEAD_DIM:r0 + NA_SLAB, :] = jnp.ones((NA_SLAB - HEAD_DIM, vn.shape[1]),
                                                             vtn_ref.dtype)


def _inproj(x, mod, g, w, wt, cos, sin, *, ts, has_q):
    b, t, d = x.shape
    t_widths = [DIFF_HEADS * 2 * HEAD_DIM, NA_HEADS * NA_SLAB]
    nat_widths = [d] * (w.shape[1] // d)
    per_batch_mod = mod.shape[0] > 1
    kern = functools.partial(_inproj_kernel, n_rope=(DIFF_HEADS if has_q else 0),
                             q_scale=HEAD_DIM ** -0.5 * LOG2E, has_q=has_q, chunk=512)
    return pl.pallas_call(
        kern,
        out_shape=[jax.ShapeDtypeStruct((b, t, wd), BF16) for wd in nat_widths]
        + [jax.ShapeDtypeStruct((b, wd, t), BF16) for wd in t_widths],
        grid=(b, t // ts),
        in_specs=[pl.BlockSpec((1, ts, d), lambda i, j: (i, j, 0)),
                  pl.BlockSpec((1,) + mod.shape[1:],
                               (lambda i, j: (i, 0, 0)) if per_batch_mod else (lambda i, j: (0, 0, 0))),
                  _const_spec(g.shape),
                  _const_spec(w.shape),
                  _const_spec(wt.shape),
                  pl.BlockSpec((ts, LANES), lambda i, j: (j, 0)),
                  pl.BlockSpec((ts, LANES), lambda i, j: (j, 0))],
        out_specs=[pl.BlockSpec((1, ts, wd), lambda i, j: (i, j, 0)) for wd in nat_widths]
        + [pl.BlockSpec((1, wd, ts), lambda i, j: (i, 0, j)) for wd in t_widths],
        compiler_params=_params("parallel", "arbitrary"),
        name="inproj_q" if has_q else "inproj_ctx",
    )(x, mod, g, w, wt, cos, sin)


def _ds(start, size):
    if isinstance(start, int):
        return pl.ds(start, size)
    return pl.ds(pl.multiple_of(start, size), size)


def _run_pipeline(n, stages, n_sets, issue_order=None, static_edge=0):
    k = len(stages)
    issue_order = tuple(range(k)) if issue_order is None else issue_order
    assert sorted(issue_order) == list(range(k))
    assert n % n_sets == 0 and n >= n_sets + k + 2 * static_edge

    def step(t, u):
        for i in issue_order:
            c = t - i
            if isinstance(c, int) and not 0 <= c < n:
                continue
            stages[i](c, (u - i) % n_sets)

    t0 = -(-(static_edge + k - 1) // n_sets) * n_sets
    t1 = t0 + (n - static_edge - t0) // n_sets * n_sets
    for t in range(t0):
        step(t, t % n_sets)

    def body(tt, carry):
        for u in range(n_sets):
            step(tt * n_sets + u, u)
        return carry

    lax.fori_loop(t0 // n_sets, t1 // n_sets, body, 0)
    for t in range(t1, n + k - 1):
        step(t, t % n_sets)


def _diff_kernel(q_ref, k_ref, vt_ref, kc_ref, vct_ref, lamv_ref, g_ref, o_ref,
                 kall, vtall, *hand_off, tq):
    sets = tuple(tuple(hand_off[i:i + 3]) for i in range(0, len(hand_off), 3))
    s = k_ref.shape[1]
    w = 2 * HEAD_DIM
    n_j = s // tq
    for h in range(DIFF_HEADS):
        cols = slice(h * w, (h + 1) * w)
        kall[h, 0:s, :] = k_ref[0, :, cols]
        kall[h, s:, :] = kc_ref[0, :, cols]
        vtall[h, 0:w, 0:s] = vt_ref[0, cols, :]
        vtall[h, 0:w, s:] = vct_ref[0, cols, :]
        vtall[h, w:, :] = jnp.ones((vtall.shape[1] - w, vtall.shape[2]), vtall.dtype)
    lv = lamv_ref[...]
    lam = (jnp.exp(jnp.sum(lv[0:1] * lv[1:2], axis=-1, keepdims=True))
           - jnp.exp(jnp.sum(lv[2:3] * lv[3:4], axis=-1, keepdims=True)) + LAM_INIT)

    def chunk(c):
        h, j = c // n_j, c % n_j
        return h, (_ds(j * tq, tq), _ds(h * w, w))

    def scores(c, p):
        st, mb, _ = sets[p]
        h, idx = chunk(c)
        q = q_ref[(0,) + idx]
        lane = lax.broadcasted_iota(jnp.int32, q.shape, 1)
        for i in range(2):
            sel = (lane < HEAD_DIM) if i == 0 else (lane >= HEAD_DIM)
            qi = jnp.where(sel, q, jnp.zeros_like(q))
            x = _dot_nt(kall[h], qi)
            st[i] = x
            mb[i] = jnp.max(x, axis=0, keepdims=True)

    def values(c, p):
        st, mb, ot = sets[p]
        h, _ = chunk(c)
        for i in range(2):
            e = jnp.exp2(st[i] - mb[i]).astype(BF16)
            ot[i] = _dot(vtall[h], e)

    def finish(c, p):
        _, _, ot = sets[p]
        _, idx = chunk(c)
        o1 = ot[0, 0:w, :] / ot[0, w:w + 1, :]
        o2 = ot[1, 0:w, :] / ot[1, w:w + 1, :]
        o = (o1 - lam * o2).T
        o_ref[(0,) + idx] = (_rms(o) * g_ref[...] * (1.0 - LAM_INIT)).astype(o_ref.dtype)

    _run_pipeline(DIFF_HEADS * n_j, (scores, values, finish), len(sets), DIFF_ORDER)


def _diff_attention(q, k, vt, kc, vct, lamv, g, *, tq):
    b, s, d = q.shape
    c = kc.shape[1]
    w = 2 * HEAD_DIM
    half = DIFF_HEADS * w
    ones_rows = 2 * SUBLANES
    hand_off = [pltpu.VMEM((2, s + c, tq), F32), pltpu.VMEM((2, 1, tq), F32),
                pltpu.VMEM((2, w + ones_rows, tq), F32)]
    return pl.pallas_call(
        functools.partial(_diff_kernel, tq=tq),
        out_shape=jax.ShapeDtypeStruct((b, s, half), BF16),
        grid=(b,),
        in_specs=[pl.BlockSpec((1, s, half), lambda i: (i, 0, 0)),
                  pl.BlockSpec((1, s, half), lambda i: (i, 0, 0)),
                  pl.BlockSpec((1, half, s), lambda i: (i, 0, 0)),
                  pl.BlockSpec((1, c, half), lambda i: (i, 0, 0)),
                  pl.BlockSpec((1, half, c), lambda i: (i, 0, 0)),
                  pl.BlockSpec(lamv.shape, lambda i: (0, 0)),
                  pl.BlockSpec(g.shape, lambda i: (0, 0))],
        out_specs=pl.BlockSpec((1, s, half), lambda i: (i, 0, 0)),
        scratch_shapes=[pltpu.VMEM((DIFF_HEADS, s + c, w), BF16),
                        pltpu.VMEM((DIFF_HEADS, w + ones_rows, s + c), BF16)] + hand_off * 2,
        compiler_params=_params("parallel"),
        name="diff_attn",
    )(q, k, vt, kc, vct, lamv, g)


def _natten_kernel(q_ref, k_ref, vt_ref, kc_ref, vct_ref, toep_ref, o_ref,
                   bias_tab, on_buf, *hand_off, rows, cases, case_ids):
    sets = tuple(tuple(hand_off[i:i + 4]) for i in range(0, len(hand_off), 4))
    n_lat = NA_KROWS * GRID_W
    n_q = NA_QROWS * GRID_W
    n_pairs = NA_QROWS // 2
    live = [[[x for x in range(NA_KROWS)
              if max(pat[x * NA_QROWS + 2 * rp], pat[x * NA_QROWS + 2 * rp + 1]) >= 0]
             for rp in range(n_pairs)] for pat in cases]
    common = max(set(case_ids), key=case_ids.count)
    edge_blocks = max(case_ids.index(common), case_ids[::-1].index(common))
    assert all(ci == common for ci in case_ids[edge_blocks:len(case_ids) - edge_blocks])

    def case_of(c):
        return case_ids[c // NA_HEADS] if isinstance(c, int) else common

    def block(x, rp):
        return slice(x * GRID_W, (x + 1) * GRID_W), slice(rp * LANES, (rp + 1) * LANES)

    @pl.when(pl.program_id(0) == 0)
    def _():
        lane = lax.broadcasted_iota(jnp.int32, (GRID_W, LANES), 1)
        masked = jnp.full((GRID_W, LANES), NEG_INF, F32)
        for ci, pat in enumerate(cases):
            for x in range(NA_KROWS):
                for rp in range(NA_QROWS // 2):
                    d0, d1 = pat[x * NA_QROWS + 2 * rp], pat[x * NA_QROWS + 2 * rp + 1]
                    for h in range(NA_HEADS):
                        left = toep_ref[h, d0] if d0 >= 0 else masked
                        right = toep_ref[h, d1] if d1 >= 0 else masked
                        blk = masked if max(d0, d1) < 0 else jnp.where(lane < GRID_W, left, right)
                        bias_tab[ci, h, x * GRID_W:(x + 1) * GRID_W, rp * LANES:(rp + 1) * LANES] = blk

    def item(c):
        rb, h = c // NA_HEADS, c % NA_HEADS
        ustart = rb * NA_QROWS - NA_KH // 2
        if isinstance(c, int):
            k_rows = pl.ds(min(max(ustart, 0), rows - NA_KROWS) * GRID_W, n_lat)
            slab = pl.ds(h * NA_SLAB, NA_SLAB)
        else:
            k_rows = pl.ds(pl.multiple_of(jnp.clip(ustart, 0, rows - NA_KROWS) * GRID_W, n_q), n_lat)
            slab = pl.ds(pl.multiple_of(h * NA_SLAB, 2 * SUBLANES), NA_SLAB)
        return rb, h, k_rows, _ds(rb * n_q, n_q), _ds((h // 2) * LANES, LANES), slab

    def scores(c, p):
        st = sets[p][0]
        rb, h, k_rows, q_rows, pair_cols, _ = item(c)
        qp = q_ref[0, q_rows, pair_cols]
        lane = lax.broadcasted_iota(jnp.int32, qp.shape, 1)
        sel = (lane < HEAD_DIM) if p % 2 == 0 else (lane >= HEAD_DIM)
        qh = jnp.where(sel, qp, jnp.zeros_like(qp))
        x_lat = _dot_nt(k_ref[0, k_rows, pair_cols], qh)
        ci = case_of(c)
        for rp in range(n_pairs):
            for x in live[ci][rp]:
                bk = block(x, rp)
                st[bk] = x_lat[bk] + bias_tab[(ci, h) + bk]
        st[n_lat:, :] = _dot_nt(kc_ref[0, :, pair_cols], qh)

    def col_max(c, p):
        st, mb, _, _ = sets[p]
        ci = case_of(c)
        for rp in range(n_pairs):
            lanes = block(0, rp)[1]
            m = jnp.max(st[n_lat:, lanes], axis=0, keepdims=True)
            for x in live[ci][rp]:
                m = jnp.maximum(m, jnp.max(st[block(x, rp)], axis=0, keepdims=True))
            mb[:, lanes] = m

    def exps(c, p):
        st, mb, eb, _ = sets[p]
        ci = case_of(c)
        for rp in range(n_pairs):
            lanes = block(0, rp)[1]
            m = mb[:, lanes]
            for x in range(NA_KROWS):
                bk = block(x, rp)
                if x in live[ci][rp]:
                    eb[bk] = jnp.exp2((st[bk] - m).astype(BF16))
                else:
                    eb[bk] = jnp.exp2(jnp.full((GRID_W, LANES), NEG_INF, F32) - m).astype(BF16)
            eb[n_lat:, lanes] = jnp.exp2((st[n_lat:, lanes] - m).astype(BF16))

    def values(c, p):
        _, _, eb, ot = sets[p]
        _, _, k_rows, _, _, slab = item(c)
        ot[...] = (_dot(vt_ref[0, slab, k_rows], eb[0:n_lat, :])
                   + _dot(vct_ref[0, slab, :], eb[n_lat:, :]))

    def finish(c, p):
        _, _, _, ot = sets[p]
        _, _, _, q_rows, pair_cols, _ = item(c)
        half = p % 2
        on_buf[half * HEAD_DIM:(half + 1) * HEAD_DIM, :] = (
            ot[0:HEAD_DIM, :] / ot[HEAD_DIM:HEAD_DIM + 1, :])
        if half == 1:
            o_ref[0, q_rows, pair_cols] = on_buf[...].T.astype(o_ref.dtype)

    assert len(sets) % 2 == 0
    _run_pipeline((rows // NA_QROWS) * NA_HEADS, (scores, col_max, exps, values, finish), len(sets),
                  NA_ORDER, static_edge=edge_blocks * NA_HEADS)


def _na_block_patterns(rows):
    pats = []
    for rb in range(rows // NA_QROWS):
        ustart = int(np.clip(rb * NA_QROWS - NA_KH // 2, 0, rows - NA_KROWS))
        pat = []
        for x in range(NA_KROWS):
            for rr in range(NA_QROWS):
                r = rb * NA_QROWS + rr
                start = int(np.clip(r - NA_KH // 2, 0, rows - NA_KH))
                a = ustart + x
                pat.append(a - r + NA_KH - 1 if start <= a < start + NA_KH else -1)
        pats.append(tuple(pat))
    return pats


def _na_block_cases(rows):
    pats = _na_block_patterns(rows)
    cases = tuple(sorted(set(pats), key=pats.index))
    return cases, tuple(cases.index(p) for p in pats)


def _na_toeplitz(rpb):
    jq = np.arange(GRID_W)
    col_start = np.clip(jq - NA_KW // 2, 0, GRID_W - NA_KW)
    in_win = (jq[None, :] >= col_start[:, None]) & (jq[None, :] < col_start[:, None] + NA_KW)
    dc_idx = np.clip(jq[None, :] - jq[:, None] + NA_KW - 1, 0, 2 * NA_KW - 2)
    onehot = (dc_idx.T[None] == np.arange(2 * NA_KW - 1)[:, None, None]) & in_win.T[None]
    toep = jnp.einsum("hdj,jkq->hdkq", rpb, jnp.asarray(onehot, F32),
                      precision=lax.Precision.HIGHEST)
    toep = jnp.where(jnp.asarray(in_win.T)[None, None], toep * LOG2E, NEG_INF)
    return jnp.concatenate([toep, toep], axis=-1)


def _natten(q, k, vt, kc, vct, toep, cases, case_ids):
    b, s, d = q.shape
    c = kc.shape[1]
    rows = s // GRID_W
    w = NA_HEADS * HEAD_DIM
    n_q = NA_QROWS * GRID_W
    n_keys = NA_KROWS * GRID_W + c
    hand_off = [pltpu.VMEM((n_keys, n_q), F32), pltpu.VMEM((1, n_q), F32),
                pltpu.VMEM((n_keys, n_q), BF16),
                pltpu.VMEM((NA_SLAB, n_q), F32)]
    return pl.pallas_call(
        functools.partial(_natten_kernel, rows=rows, cases=cases, case_ids=case_ids),
        out_shape=jax.ShapeDtypeStruct((b, s, w), BF16),
        grid=(b,),
        in_specs=[pl.BlockSpec((1, s, w), lambda i: (i, 0, 1)),
                  pl.BlockSpec((1, s, w), lambda i: (i, 0, 1)),
                  pl.BlockSpec((1,) + vt.shape[1:], lambda i: (i, 0, 0)),
                  pl.BlockSpec((1, c, w), lambda i: (i, 0, 1)),
                  pl.BlockSpec((1,) + vct.shape[1:], lambda i: (i, 0, 0)),
                  _const_spec(toep.shape)],
        out_specs=pl.BlockSpec((1, s, w), lambda i: (i, 0, 0)),
        scratch_shapes=[pltpu.VMEM((len(cases), NA_HEADS, NA_KROWS * GRID_W, n_q), F32),
                        pltpu.VMEM((LANES, n_q), F32)] + hand_off * NA_SETS,
        compiler_params=_params("arbitrary"),
        name="natten",
    )(q, k, vt, kc, vct, toep)


def _tail_kernel(x_ref, od_ref, on_ref, mod_ref, wo_ref, g2_ref, w1_ref, w2_ref, gf_ref, o_ref,
                 *, ff_chunk):
    half = od_ref.shape[-1]
    gate_a = mod_ref[0, 2:3, :]
    shift_m = mod_ref[0, 3:4, :]
    scale_m = mod_ref[0, 4:5, :]
    gate_m = mod_ref[0, 5:6, :]
    attn = _dot(od_ref[0], wo_ref[:half, :]) + _dot(on_ref[0], wo_ref[half:, :])
    x1 = x_ref[0] + gate_a * attn
    h2 = ((_rms(x1) * g2_ref[...]) * (1.0 + scale_m) + shift_m).astype(BF16)
    y = None
    for c in range(w1_ref.shape[1] // ff_chunk):
        cols = slice(c * ff_chunk, (c + 1) * ff_chunk)
        a = jnp.maximum(_dot(h2, w1_ref[:, cols]), 0.0)
        part = _dot((a * a).astype(BF16), w2_ref[cols, :])
        y = part if y is None else y + part
    x2 = x1 + gate_m * y
    o_ref[0] = _rms(x2) * gf_ref[...]


def _tail(x, o_d, o_n, mod, w_out, g2, w1, w2, gf, *, ts, ff_chunk=1024):
    b, s, d = x.shape
    half = o_d.shape[-1]
    return pl.pallas_call(
        functools.partial(_tail_kernel, ff_chunk=ff_chunk),
        out_shape=jax.ShapeDtypeStruct((b, s, d), F32),
        grid=(b, s // ts),
        in_specs=[pl.BlockSpec((1, ts, d), lambda i, j: (i, j, 0)),
                  pl.BlockSpec((1, ts, half), lambda i, j: (i, j, 0)),
                  pl.BlockSpec((1, ts, half), lambda i, j: (i, j, 0)),
                  pl.BlockSpec((1,) + mod.shape[1:], lambda i, j: (i, 0, 0)),
                  _const_spec(w_out.shape),
                  _const_spec(g2.shape),
                  _const_spec(w1.shape),
                  _const_spec(w2.shape),
                  _const_spec(gf.shape)],
        out_specs=pl.BlockSpec((1, ts, d), lambda i, j: (i, j, 0)),
        compiler_params=_params("parallel", "arbitrary"),
        name="tail",
    )(x, o_d, o_n, mod, w_out, g2, w1, w2, gf)


def _rope_tables(s):
    pos = np.arange(s)
    inv = ROPE_BASE ** (-np.arange(ROPE_FREQS, dtype=np.float32) / ROPE_FREQS)
    ang_r = (pos // GRID_W).astype(np.float32)[:, None] * inv
    ang_c = (pos % GRID_W).astype(np.float32)[:, None] * inv
    cos = np.concatenate([np.cos(ang_r)] * 2 + [np.cos(ang_c)] * 2, axis=1)
    sin = np.concatenate([-np.sin(ang_r), np.sin(ang_r), -np.sin(ang_c), np.sin(ang_c)], axis=1)
    reps = LANES // HEAD_DIM
    return (jnp.asarray(np.tile(cos, (1, reps)), F32), jnp.asarray(np.tile(sin, (1, reps)), F32))


def _tiles(s):
    return min(1024, s), min(512, s), min(256, s)


def kernel(x, c, ctx, c_ctx, w_mod, b_mod, norm1_g, w_in, lam_q1, lam_k1, lam_q2, lam_k2,
           diff_subln_g, na_rpb, w_out, norm2_g, w_fc1, w_fc2, final_g):
    b, s, d = x.shape
    rows = s // GRID_W
    assert w_mod.shape[0] == 1, "single-layer block"
    assert s % (GRID_W * NA_QROWS) == 0 and rows >= NA_KROWS
    ts_proj, ts_tail, tq = _tiles(s)

    n_cond = b + 1
    pad = (-n_cond) % SUBLANES
    cond = jnp.concatenate([c, c_ctx[None, :], jnp.zeros((pad, d), F32)], axis=0)
    mod = _mod(cond, w_mod[0], b_mod[0][None, :])
    mod_x = mod[:b].reshape(b, 6, d)
    mod_c = mod[b:b + 1].reshape(1, 6, d)

    w_in_b = w_in[0].astype(BF16)
    w_qk, w_v_t = w_in_b[:, :2 * d], w_in_b[:, 2 * d:].T
    cos, sin = _rope_tables(s)
    g1 = norm1_g[0][None, :]
    q, k, vt_d, vt_n = _inproj(x, mod_x, g1, w_qk, w_v_t, cos, sin, ts=ts_proj, has_q=True)
    n_ctx = ctx.shape[1]
    kc, vct_d, vct_n = _inproj(ctx, mod_c, g1, w_qk[:, d:], w_v_t, cos[:n_ctx], sin[:n_ctx],
                               ts=n_ctx, has_q=False)

    lamv = jnp.stack([lam_q1[0], lam_k1[0], lam_q2[0], lam_k2[0]])
    o_d = _diff_attention(q, k, vt_d, kc, vct_d, lamv, diff_subln_g[0][None, :], tq=tq)
    cases, case_ids = _na_block_cases(rows)
    o_n = _natten(q, k, vt_n, kc, vct_n, _na_toeplitz(na_rpb[0]), cases, case_ids)

    return _tail(x, o_d, o_n, mod_x, w_out[0].astype(BF16), norm2_g[0][None, :],
                 w_fc1[0].astype(BF16), w_fc2[0].astype(BF16), final_g[None, :], ts=ts_tail)
```

```python
import functools
import math

import numpy as np
import jax
import jax.numpy as jnp
from jax import lax
from jax.experimental import pallas as pl
from jax.experimental.pallas import tpu as pltpu

GRID_W = 64
HEAD_DIM = 64
DIFF_HEADS = 4
NA_HEADS = 8
NA_KH = 8
NA_KW = 16
ROPE_BASE = 10000.0
ROPE_FREQS = HEAD_DIM // 4
EPS = 1e-6
NEG_INF = -1e30
LAM_INIT = 0.8 - 0.6 * math.exp(-0.3 * 0)
LOG2E = math.log2(math.e)

LANES = 128
SUBLANES = 8
VMEM_LIMIT = 56 * 1024 * 1024

NA_QROWS = 4
NA_KROWS = 12
NA_SLAB = HEAD_DIM + 16
NA_SETS = 4
DIFF_ORDER = (0, 1, 2)
MAX_UNSHIFTED_SCORE = 60.0
SCORE_BOUND_MARGIN = 1.05
NA_ORDER = (3, 0, 1, 2, 4)

BF16 = jnp.bfloat16
F32 = jnp.float32


def _params(*sem):
    return pltpu.CompilerParams(dimension_semantics=sem, vmem_limit_bytes=VMEM_LIMIT)


def _const_spec(shape):
    nd = len(shape)
    return pl.BlockSpec(shape, lambda *_: (0,) * nd, pipeline_mode=pl.Buffered(1))


def _dot(a, b):
    return jnp.dot(a, b, preferred_element_type=F32)


def _dot_nt(a, b):
    return lax.dot_general(a, b, (((1,), (1,)), ((), ())), preferred_element_type=F32)


def _rms(xf):
    return xf * lax.rsqrt(jnp.mean(xf * xf, axis=-1, keepdims=True) + EPS)


def _mod_kernel(cond_ref, w_ref, b_ref, o_ref):
    cnd = cond_ref[...]
    act = (cnd * jax.nn.sigmoid(cnd)).astype(BF16)
    o_ref[...] = _dot(act, w_ref[...].astype(BF16)) + b_ref[...]


def _mod(cond, w_mod, b_mod, tn=768):
    r, d = cond.shape
    n = w_mod.shape[1]
    return pl.pallas_call(
        _mod_kernel,
        out_shape=jax.ShapeDtypeStruct((r, n), F32),
        grid=(n // tn,),
        in_specs=[pl.BlockSpec((r, d), lambda j: (0, 0)),
                  pl.BlockSpec((d, tn), lambda j: (0, j)),
                  pl.BlockSpec((1, tn), lambda j: (0, j))],
        out_specs=pl.BlockSpec((r, tn), lambda j: (0, j)),
        compiler_params=_params("arbitrary"),
        name="mod",
    )(cond, w_mod, b_mod)


def _rope(xf, cos, sin_signed):
    lane = lax.broadcasted_iota(jnp.int32, xf.shape, 1)
    first = (lane % (2 * ROPE_FREQS)) < ROPE_FREQS
    partner = jnp.where(first,
                        pltpu.roll(xf, LANES - ROPE_FREQS, 1),
                        pltpu.roll(xf, ROPE_FREQS, 1))
    return xf * cos + partner * sin_signed


def _inproj_kernel(x_ref, mod_ref, g_ref, w_ref, wt_ref, cos_ref, sin_ref, *out_refs,
                   n_rope, q_scale, has_q, chunk):
    xf = x_ref[0]
    shift = mod_ref[0, 0:1, :]
    scale = mod_ref[0, 1:2, :]
    h = (_rms(xf) * g_ref[...]) * (1.0 + scale) + shift
    hb = h.astype(BF16)
    nat_refs, vtd_ref, vtn_ref = out_refs[:-2], out_refs[-2], out_refs[-1]
    col0 = 0
    for oi, o_ref in enumerate(nat_refs):
        width = o_ref.shape[-1]
        is_q = has_q and oi == 0
        roped = has_q and oi < 2
        for c0 in range(0, width, chunk):
            p = _dot(hb, w_ref[:, col0 + c0:col0 + c0 + chunk])
            for c in range(c0 // LANES, (c0 + chunk) // LANES):
                pc = p[:, c * LANES - c0:(c + 1) * LANES - c0]
                if roped and c < n_rope:
                    pc = _rope(pc, cos_ref[...], sin_ref[...])
                if is_q:
                    pc = pc * q_scale
                o_ref[0, :, c * LANES:(c + 1) * LANES] = pc.astype(o_ref.dtype)
        col0 += width
    n_d = vtd_ref.shape[1]
    vtd_ref[0] = _dot_nt(wt_ref[0:n_d, :], hb).astype(vtd_ref.dtype)
    vn = _dot_nt(wt_ref[n_d:, :], hb)
    for hd in range(NA_HEADS):
        r0 = hd * NA_SLAB
        vtn_ref[0, r0:r0 + HEAD_DIM, :] = vn[hd * HEAD_DIM:(hd + 1) * HEAD_DIM].astype(vtn_ref.dtype)
        vtn_ref[0, r0 + HEAD_DIM:r0 + NA_SLAB, :] = jnp.ones((NA_SLAB - HEAD_DIM, vn.shape[1]),
                                                             vtn_ref.dtype)


def _inproj(x, mod, g, w, wt, cos, sin, *, ts, has_q):
    b, t, d = x.shape
    t_widths = [DIFF_HEADS * 2 * HEAD_DIM, NA_HEADS * NA_SLAB]
    nat_widths = [d] * (w.shape[1] // d)
    per_batch_mod = mod.shape[0] > 1
    kern = functools.partial(_inproj_kernel, n_rope=(DIFF_HEADS if has_q else 0),
                             q_scale=HEAD_DIM ** -0.5 * LOG2E, has_q=has_q, chunk=512)
    return pl.pallas_call(
        kern,
        out_shape=[jax.ShapeDtypeStruct((b, t, wd), BF16) for wd in nat_widths]
        + [jax.ShapeDtypeStruct((b, wd, t), BF16) for wd in t_widths],
        grid=(b, t // ts),
        in_specs=[pl.BlockSpec((1, ts, d), lambda i, j: (i, j, 0)),
                  pl.BlockSpec((1,) + mod.shape[1:],
                               (lambda i, j: (i, 0, 0)) if per_batch_mod else (lambda i, j: (0, 0, 0))),
                  _const_spec(g.shape),
                  _const_spec(w.shape),
                  _const_spec(wt.shape),
                  pl.BlockSpec((ts, LANES), lambda i, j: (j, 0)),
                  pl.BlockSpec((ts, LANES), lambda i, j: (j, 0))],
        out_specs=[pl.BlockSpec((1, ts, wd), lambda i, j: (i, j, 0)) for wd in nat_widths]
        + [pl.BlockSpec((1, wd, ts), lambda i, j: (i, 0, j)) for wd in t_widths],
        compiler_params=_params("parallel", "arbitrary"),
        name="inproj_q" if has_q else "inproj_ctx",
    )(x, mod, g, w, wt, cos, sin)


def _ds(start, size):
    if isinstance(start, int):
        return pl.ds(start, size)
    return pl.ds(pl.multiple_of(start, size), size)


def _run_pipeline(n, stages, n_sets, issue_order=None, static_edge=0):
    k = len(stages)
    issue_order = tuple(range(k)) if issue_order is None else issue_order
    assert sorted(issue_order) == list(range(k))
    assert n % n_sets == 0 and n >= n_sets + k + 2 * static_edge

    def step(t, u):
        for i in issue_order:
            c = t - i
            if isinstance(c, int) and not 0 <= c < n:
                continue
            stages[i](c, (u - i) % n_sets)

    t0 = -(-(static_edge + k - 1) // n_sets) * n_sets
    t1 = t0 + (n - static_edge - t0) // n_sets * n_sets
    for t in range(t0):
        step(t, t % n_sets)

    def body(tt, carry):
        for u in range(n_sets):
            step(tt * n_sets + u, u)
        return carry

    lax.fori_loop(t0 // n_sets, t1 // n_sets, body, 0)
    for t in range(t1, n + k - 1):
        step(t, t % n_sets)


def _diff_kernel(q_ref, k_ref, vt_ref, kc_ref, vct_ref, lamv_ref, g_ref, o_ref,
                 kall, vtall, eb_a, eb_b, *hand_off, tq):
    sets = tuple(tuple(hand_off[i:i + 3]) for i in range(0, len(hand_off), 3))
    ebs = (eb_a, eb_b)
    s = k_ref.shape[1]
    w = 2 * HEAD_DIM
    n_j = s // tq
    for h in range(DIFF_HEADS):
        cols = slice(h * w, (h + 1) * w)
        kall[h, 0:s, :] = k_ref[0, :, cols]
        kall[h, s:, :] = kc_ref[0, :, cols]
        vtall[h, 0:w, 0:s] = vt_ref[0, cols, :]
        vtall[h, 0:w, s:] = vct_ref[0, cols, :]
        vtall[h, w:, :] = jnp.ones((vtall.shape[1] - w, vtall.shape[2]), vtall.dtype)
    lv = lamv_ref[...]
    lam = (jnp.exp(jnp.sum(lv[0:1] * lv[1:2], axis=-1, keepdims=True))
           - jnp.exp(jnp.sum(lv[2:3] * lv[3:4], axis=-1, keepdims=True)) + LAM_INIT)

    def chunk(c):
        h, j = c // n_j, c % n_j
        return h, (_ds(j * tq, tq), _ds(h * w, w))

    def scores(c, p):
        st, mb, _ = sets[p]
        h, idx = chunk(c)
        q = q_ref[(0,) + idx]
        lane = lax.broadcasted_iota(jnp.int32, q.shape, 1)
        for i in range(2):
            sel = (lane < HEAD_DIM) if i == 0 else (lane >= HEAD_DIM)
            qi = jnp.where(sel, q, jnp.zeros_like(q))
            x = _dot_nt(kall[h], qi)
            st[i] = x
            mb[i] = jnp.max(x, axis=0, keepdims=True)

    def values(c, p):
        st, mb, ot = sets[p]
        h, _ = chunk(c)
        for i in range(2):
            e = jnp.exp2(st[i] - mb[i]).astype(BF16)
            ot[i] = _dot(vtall[h], e)

    def finish(c, p):
        _, _, ot = sets[p]
        _, idx = chunk(c)
        o1 = ot[0, 0:w, :] / ot[0, w:w + 1, :]
        o2 = ot[1, 0:w, :] / ot[1, w:w + 1, :]
        o = (o1 - lam * o2).T
        o_ref[(0,) + idx] = (_rms(o) * g_ref[...] * (1.0 - LAM_INIT)).astype(o_ref.dtype)

    def scores_unshifted(c, p):
        h, idx = chunk(c)
        q = q_ref[(0,) + idx]
        lane = lax.broadcasted_iota(jnp.int32, q.shape, 1)
        for i in range(2):
            sel = (lane < HEAD_DIM) if i == 0 else (lane >= HEAD_DIM)
            qi = jnp.where(sel, q, jnp.zeros_like(q))
            ebs[p][i] = jnp.exp2(_dot_nt(kall[h], qi)).astype(BF16)

    def values_unshifted(c, p):
        ot = sets[p][2]
        h, _ = chunk(c)
        for i in range(2):
            ot[i] = _dot(vtall[h], ebs[p][i])

    d_half = q_ref.shape[2]
    grp = (lax.broadcasted_iota(jnp.int32, (d_half, LANES), 0) // HEAD_DIM
           == lax.broadcasted_iota(jnp.int32, (d_half, LANES), 1)).astype(BF16)

    def max_norm2(x):
        xf = x.astype(F32)
        return jnp.max(_dot((xf * xf).astype(BF16), grp))

    k_norm2 = jnp.maximum(max_norm2(k_ref[0]), max_norm2(kc_ref[0]))
    bounded = max_norm2(q_ref[0]) * k_norm2 * SCORE_BOUND_MARGIN < MAX_UNSHIFTED_SCORE ** 2

    n_items = DIFF_HEADS * n_j
    lax.cond(bounded,
             lambda: _run_pipeline(n_items, (scores_unshifted, values_unshifted, finish), len(sets)),
             lambda: _run_pipeline(n_items, (scores, values, finish), len(sets), DIFF_ORDER))


def _diff_attention(q, k, vt, kc, vct, lamv, g, *, tq):
    b, s, d = q.shape
    c = kc.shape[1]
    w = 2 * HEAD_DIM
    half = DIFF_HEADS * w
    ones_rows = 2 * SUBLANES
    hand_off = [pltpu.VMEM((2, s + c, tq), F32), pltpu.VMEM((2, 1, tq), F32),
                pltpu.VMEM((2, w + ones_rows, tq), F32)]
    return pl.pallas_call(
        functools.partial(_diff_kernel, tq=tq),
        out_shape=jax.ShapeDtypeStruct((b, s, half), BF16),
        grid=(b,),
        in_specs=[pl.BlockSpec((1, s, half), lambda i: (i, 0, 0)),
                  pl.BlockSpec((1, s, half), lambda i: (i, 0, 0)),
                  pl.BlockSpec((1, half, s), lambda i: (i, 0, 0)),
                  pl.BlockSpec((1, c, half), lambda i: (i, 0, 0)),
                  pl.BlockSpec((1, half, c), lambda i: (i, 0, 0)),
                  pl.BlockSpec(lamv.shape, lambda i: (0, 0)),
                  pl.BlockSpec(g.shape, lambda i: (0, 0))],
        out_specs=pl.BlockSpec((1, s, half), lambda i: (i, 0, 0)),
        scratch_shapes=[pltpu.VMEM((DIFF_HEADS, s + c, w), BF16),
                        pltpu.VMEM((DIFF_HEADS, w + ones_rows, s + c), BF16),
                        pltpu.VMEM((2, s + c, tq), BF16),
                        pltpu.VMEM((2, s + c, tq), BF16)] + hand_off * 2,
        compiler_params=_params("parallel"),
        name="diff_attn",
    )(q, k, vt, kc, vct, lamv, g)


def _natten_kernel(q_ref, k_ref, vt_ref, kc_ref, vct_ref, toep_ref, o_ref,
                   bias_tab, on_buf, *hand_off, rows, cases, case_ids):
    sets = tuple(tuple(hand_off[i:i + 4]) for i in range(0, len(hand_off), 4))
    n_lat = NA_KROWS * GRID_W
    n_q = NA_QROWS * GRID_W
    n_pairs = NA_QROWS // 2
    live = [[[x for x in range(NA_KROWS)
              if max(pat[x * NA_QROWS + 2 * rp], pat[x * NA_QROWS + 2 * rp + 1]) >= 0]
             for rp in range(n_pairs)] for pat in cases]
    common = max(set(case_ids), key=case_ids.count)
    edge_blocks = max(case_ids.index(common), case_ids[::-1].index(common))
    assert all(ci == common for ci in case_ids[edge_blocks:len(case_ids) - edge_blocks])

    def case_of(c):
        return case_ids[c // NA_HEADS] if isinstance(c, int) else common

    def block(x, rp):
        return slice(x * GRID_W, (x + 1) * GRID_W), slice(rp * LANES, (rp + 1) * LANES)

    @pl.when(pl.program_id(0) == 0)
    def _():
        lane = lax.broadcasted_iota(jnp.int32, (GRID_W, LANES), 1)
        masked = jnp.full((GRID_W, LANES), NEG_INF, F32)
        for ci, pat in enumerate(cases):
            for x in range(NA_KROWS):
                for rp in range(NA_QROWS // 2):
                    d0, d1 = pat[x * NA_QROWS + 2 * rp], pat[x * NA_QROWS + 2 * rp + 1]
                    for h in range(NA_HEADS):
                        left = toep_ref[h, d0] if d0 >= 0 else masked
                        right = toep_ref[h, d1] if d1 >= 0 else masked
                        blk = masked if max(d0, d1) < 0 else jnp.where(lane < GRID_W, left, right)
                        bias_tab[ci, h, x * GRID_W:(x + 1) * GRID_W, rp * LANES:(rp + 1) * LANES] = blk

    def item(c):
        rb, h = c // NA_HEADS, c % NA_HEADS
        ustart = rb * NA_QROWS - NA_KH // 2
        if isinstance(c, int):
            k_rows = pl.ds(min(max(ustart, 0), rows - NA_KROWS) * GRID_W, n_lat)
            slab = pl.ds(h * NA_SLAB, NA_SLAB)
        else:
            k_rows = pl.ds(pl.multiple_of(jnp.clip(ustart, 0, rows - NA_KROWS) * GRID_W, n_q), n_lat)
            slab = pl.ds(pl.multiple_of(h * NA_SLAB, 2 * SUBLANES), NA_SLAB)
        return rb, h, k_rows, _ds(rb * n_q, n_q), _ds((h // 2) * LANES, LANES), slab

    def scores(c, p):
        st = sets[p][0]
        rb, h, k_rows, q_rows, pair_cols, _ = item(c)
        qp = q_ref[0, q_rows, pair_cols]
        lane = lax.broadcasted_iota(jnp.int32, qp.shape, 1)
        sel = (lane < HEAD_DIM) if p % 2 == 0 else (lane >= HEAD_DIM)
        qh = jnp.where(sel, qp, jnp.zeros_like(qp))
        x_lat = _dot_nt(k_ref[0, k_rows, pair_cols], qh)
        ci = case_of(c)
        for rp in range(n_pairs):
            for x in live[ci][rp]:
                bk = block(x, rp)
                st[bk] = x_lat[bk] + bias_tab[(ci, h) + bk]
        st[n_lat:, :] = _dot_nt(kc_ref[0, :, pair_cols], qh)

    def col_max(c, p):
        st, mb, _, _ = sets[p]
        ci = case_of(c)
        for rp in range(n_pairs):
            lanes = block(0, rp)[1]
            m = jnp.max(st[n_lat:, lanes], axis=0, keepdims=True)
            for x in live[ci][rp]:
                m = jnp.maximum(m, jnp.max(st[block(x, rp)], axis=0, keepdims=True))
            mb[:, lanes] = m

    def exps(c, p):
        st, mb, eb, _ = sets[p]
        ci = case_of(c)
        for rp in range(n_pairs):
            lanes = block(0, rp)[1]
            m = mb[:, lanes]
            for x in range(NA_KROWS):
                bk = block(x, rp)
                if x in live[ci][rp]:
                    eb[bk] = jnp.exp2((st[bk] - m).astype(BF16))
                else:
                    eb[bk] = jnp.exp2(jnp.full((GRID_W, LANES), NEG_INF, F32) - m).astype(BF16)
            eb[n_lat:, lanes] = jnp.exp2((st[n_lat:, lanes] - m).astype(BF16))

    def values(c, p):
        _, _, eb, ot = sets[p]
        _, _, k_rows, _, _, slab = item(c)
        ot[...] = (_dot(vt_ref[0, slab, k_rows], eb[0:n_lat, :])
                   + _dot(vct_ref[0, slab, :], eb[n_lat:, :]))

    def finish(c, p):
        _, _, _, ot = sets[p]
        _, _, _, q_rows, pair_cols, _ = item(c)
        half = p % 2
        on_buf[half * HEAD_DIM:(half + 1) * HEAD_DIM, :] = (
            ot[0:HEAD_DIM, :] / ot[HEAD_DIM:HEAD_DIM + 1, :])
        if half == 1:
            o_ref[0, q_rows, pair_cols] = on_buf[...].T.astype(o_ref.dtype)

    assert len(sets) % 2 == 0
    _run_pipeline((rows // NA_QROWS) * NA_HEADS, (scores, col_max, exps, values, finish), len(sets),
                  NA_ORDER, static_edge=edge_blocks * NA_HEADS)


def _na_block_patterns(rows):
    pats = []
    for rb in range(rows // NA_QROWS):
        ustart = int(np.clip(rb * NA_QROWS - NA_KH // 2, 0, rows - NA_KROWS))
        pat = []
        for x in range(NA_KROWS):
            for rr in range(NA_QROWS):
                r = rb * NA_QROWS + rr
                start = int(np.clip(r - NA_KH // 2, 0, rows - NA_KH))
                a = ustart + x
                pat.append(a - r + NA_KH - 1 if start <= a < start + NA_KH else -1)
        pats.append(tuple(pat))
    return pats


def _na_block_cases(rows):
    pats = _na_block_patterns(rows)
    cases = tuple(sorted(set(pats), key=pats.index))
    return cases, tuple(cases.index(p) for p in pats)


def _na_toeplitz(rpb):
    jq = np.arange(GRID_W)
    col_start = np.clip(jq - NA_KW // 2, 0, GRID_W - NA_KW)
    in_win = (jq[None, :] >= col_start[:, None]) & (jq[None, :] < col_start[:, None] + NA_KW)
    dc_idx = np.clip(jq[None, :] - jq[:, None] + NA_KW - 1, 0, 2 * NA_KW - 2)
    onehot = (dc_idx.T[None] == np.arange(2 * NA_KW - 1)[:, None, None]) & in_win.T[None]
    toep = jnp.einsum("hdj,jkq->hdkq", rpb, jnp.asarray(onehot, F32),
                      precision=lax.Precision.HIGHEST)
    toep = jnp.where(jnp.asarray(in_win.T)[None, None], toep * LOG2E, NEG_INF)
    return jnp.concatenate([toep, toep], axis=-1)


def _natten(q, k, vt, kc, vct, toep, cases, case_ids):
    b, s, d = q.shape
    c = kc.shape[1]
    rows = s // GRID_W
    w = NA_HEADS * HEAD_DIM
    n_q = NA_QROWS * GRID_W
    n_keys = NA_KROWS * GRID_W + c
    hand_off = [pltpu.VMEM((n_keys, n_q), F32), pltpu.VMEM((1, n_q), F32),
                pltpu.VMEM((n_keys, n_q), BF16),
                pltpu.VMEM((NA_SLAB, n_q), F32)]
    return pl.pallas_call(
        functools.partial(_natten_kernel, rows=rows, cases=cases, case_ids=case_ids),
        out_shape=jax.ShapeDtypeStruct((b, s, w), BF16),
        grid=(b,),
        in_specs=[pl.BlockSpec((1, s, w), lambda i: (i, 0, 1)),
                  pl.BlockSpec((1, s, w), lambda i: (i, 0, 1)),
                  pl.BlockSpec((1,) + vt.shape[1:], lambda i: (i, 0, 0)),
                  pl.BlockSpec((1, c, w), lambda i: (i, 0, 1)),
                  pl.BlockSpec((1,) + vct.shape[1:], lambda i: (i, 0, 0)),
                  _const_spec(toep.shape)],
        out_specs=pl.BlockSpec((1, s, w), lambda i: (i, 0, 0)),
        scratch_shapes=[pltpu.VMEM((len(cases), NA_HEADS, NA_KROWS * GRID_W, n_q), F32),
                        pltpu.VMEM((LANES, n_q), F32)] + hand_off * NA_SETS,
        compiler_params=_params("arbitrary"),
        name="natten",
    )(q, k, vt, kc, vct, toep)


def _tail_kernel(x_ref, od_ref, on_ref, mod_ref, wo_ref, g2_ref, w1_ref, w2_ref, gf_ref, o_ref,
                 *, ff_chunk):
    half = od_ref.shape[-1]
    gate_a = mod_ref[0, 2:3, :]
    shift_m = mod_ref[0, 3:4, :]
    scale_m = mod_ref[0, 4:5, :]
    gate_m = mod_ref[0, 5:6, :]
    attn = _dot(od_ref[0], wo_ref[:half, :]) + _dot(on_ref[0], wo_ref[half:, :])
    x1 = x_ref[0] + gate_a * attn
    h2 = ((_rms(x1) * g2_ref[...]) * (1.0 + scale_m) + shift_m).astype(BF16)
    y = None
    for c in range(w1_ref.shape[1] // ff_chunk):
        cols = slice(c * ff_chunk, (c + 1) * ff_chunk)
        a = jnp.maximum(_dot(h2, w1_ref[:, cols]), 0.0)
        part = _dot((a * a).astype(BF16), w2_ref[cols, :])
        y = part if y is None else y + part
    x2 = x1 + gate_m * y
    o_ref[0] = _rms(x2) * gf_ref[...]


def _tail(x, o_d, o_n, mod, w_out, g2, w1, w2, gf, *, ts, ff_chunk=1024):
    b, s, d = x.shape
    half = o_d.shape[-1]
    return pl.pallas_call(
        functools.partial(_tail_kernel, ff_chunk=ff_chunk),
        out_shape=jax.ShapeDtypeStruct((b, s, d), F32),
        grid=(b, s // ts),
        in_specs=[pl.BlockSpec((1, ts, d), lambda i, j: (i, j, 0)),
                  pl.BlockSpec((1, ts, half), lambda i, j: (i, j, 0)),
                  pl.BlockSpec((1, ts, half), lambda i, j: (i, j, 0)),
                  pl.BlockSpec((1,) + mod.shape[1:], lambda i, j: (i, 0, 0)),
                  _const_spec(w_out.shape),
                  _const_spec(g2.shape),
                  _const_spec(w1.shape),
                  _const_spec(w2.shape),
                  _const_spec(gf.shape)],
        out_specs=pl.BlockSpec((1, ts, d), lambda i, j: (i, j, 0)),
        compiler_params=_params("parallel", "arbitrary"),
        name="tail",
    )(x, o_d, o_n, mod, w_out, g2, w1, w2, gf)


def _rope_tables(s):
    pos = np.arange(s)
    inv = ROPE_BASE ** (-np.arange(ROPE_FREQS, dtype=np.float32) / ROPE_FREQS)
    ang_r = (pos // GRID_W).astype(np.float32)[:, None] * inv
    ang_c = (pos % GRID_W).astype(np.float32)[:, None] * inv
    cos = np.concatenate([np.cos(ang_r)] * 2 + [np.cos(ang_c)] * 2, axis=1)
    sin = np.concatenate([-np.sin(ang_r), np.sin(ang_r), -np.sin(ang_c), np.sin(ang_c)], axis=1)
    reps = LANES // HEAD_DIM
    return (jnp.asarray(np.tile(cos, (1, reps)), F32), jnp.asarray(np.tile(sin, (1, reps)), F32))


def _tiles(s):
    return min(1024, s), min(512, s), min(256, s)


def kernel(x, c, ctx, c_ctx, w_mod, b_mod, norm1_g, w_in, lam_q1, lam_k1, lam_q2, lam_k2,
           diff_subln_g, na_rpb, w_out, norm2_g, w_fc1, w_fc2, final_g):
    b, s, d = x.shape
    rows = s // GRID_W
    assert w_mod.shape[0] == 1, "single-layer block"
    assert s % (GRID_W * NA_QROWS) == 0 and rows >= NA_KROWS
    ts_proj, ts_tail, tq = _tiles(s)

    n_cond = b + 1
    pad = (-n_cond) % SUBLANES
    cond = jnp.concatenate([c, c_ctx[None, :], jnp.zeros((pad, d), F32)], axis=0)
    mod = _mod(cond, w_mod[0], b_mod[0][None, :])
    mod_x = mod[:b].reshape(b, 6, d)
    mod_c = mod[b:b + 1].reshape(1, 6, d)

    w_in_b = w_in[0].astype(BF16)
    w_qk, w_v_t = w_in_b[:, :2 * d], w_in_b[:, 2 * d:].T
    cos, sin = _rope_tables(s)
    g1 = norm1_g[0][None, :]
    q, k, vt_d, vt_n = _inproj(x, mod_x, g1, w_qk, w_v_t, cos, sin, ts=ts_proj, has_q=True)
    n_ctx = ctx.shape[1]
    kc, vct_d, vct_n = _inproj(ctx, mod_c, g1, w_qk[:, d:], w_v_t, cos[:n_ctx], sin[:n_ctx],
                               ts=n_ctx, has_q=False)

    lamv = jnp.stack([lam_q1[0], lam_k1[0], lam_q2[0], lam_k2[0]])
    o_d = _diff_attention(q, k, vt_d, kc, vct_d, lamv, diff_subln_g[0][None, :], tq=tq)
    cases, case_ids = _na_block_cases(rows)
    o_n = _natten(q, k, vt_n, kc, vct_n, _na_toeplitz(na_rpb[0]), cases, case_ids)

    return _tail(x, o_d, o_n, mod_x, w_out[0].astype(BF16), norm2_g[0][None, :],
                 w_fc1[0].astype(BF16), w_fc2[0].astype(BF16), final_g[None, :], ts=ts_tail)
```

```python
import functools
import math

import numpy as np
import jax
import jax.numpy as jnp
from jax import lax
from jax.experimental import pallas as pl
from jax.experimental.pallas import tpu as pltpu

GRID_W = 64
HEAD_DIM = 64
DIFF_HEADS = 4
NA_HEADS = 8
NA_KH = 8
NA_KW = 16
ROPE_BASE = 10000.0
ROPE_FREQS = HEAD_DIM // 4
EPS = 1e-6
NEG_INF = -1e30
LAM_INIT = 0.8 - 0.6 * math.exp(-0.3 * 0)
LOG2E = math.log2(math.e)

LANES = 128
SUBLANES = 8
VMEM_LIMIT = 56 * 1024 * 1024

NA_QROWS = 4
NA_KROWS = 12
NA_SLAB = HEAD_DIM + 16
NA_SETS = 4
DIFF_ORDER = (0, 1, 2)
MAX_UNSHIFTED_SCORE = 60.0
SCORE_BOUND_MARGIN = 1.05
NA_ORDER = (3, 0, 1, 2, 4)

BF16 = jnp.bfloat16
F32 = jnp.float32


def _params(*sem):
    return pltpu.CompilerParams(dimension_semantics=sem, vmem_limit_bytes=VMEM_LIMIT)


def _const_spec(shape):
    nd = len(shape)
    return pl.BlockSpec(shape, lambda *_: (0,) * nd, pipeline_mode=pl.Buffered(1))


def _dot(a, b):
    return jnp.dot(a, b, preferred_element_type=F32)


def _dot_nt(a, b):
    return lax.dot_general(a, b, (((1,), (1,)), ((), ())), preferred_element_type=F32)


def _rms(xf):
    return xf * lax.rsqrt(jnp.mean(xf * xf, axis=-1, keepdims=True) + EPS)


def _mod_kernel(cond_ref, w_ref, b_ref, o_ref):
    cnd = cond_ref[...]
    act = (cnd * jax.nn.sigmoid(cnd)).astype(BF16)
    o_ref[...] = _dot(act, w_ref[...].astype(BF16)) + b_ref[...]


def _mod(cond, w_mod, b_mod, tn=768):
    r, d = cond.shape
    n = w_mod.shape[1]
    return pl.pallas_call(
        _mod_kernel,
        out_shape=jax.ShapeDtypeStruct((r, n), F32),
        grid=(n // tn,),
        in_specs=[pl.BlockSpec((r, d), lambda j: (0, 0)),
                  pl.BlockSpec((d, tn), lambda j: (0, j)),
                  pl.BlockSpec((1, tn), lambda j: (0, j))],
        out_specs=pl.BlockSpec((r, tn), lambda j: (0, j)),
        compiler_params=_params("arbitrary"),
        name="mod",
    )(cond, w_mod, b_mod)


def _rope(xf, cos, sin_signed):
    lane = lax.broadcasted_iota(jnp.int32, xf.shape, 1)
    first = (lane % (2 * ROPE_FREQS)) < ROPE_FREQS
    partner = jnp.where(first,
                        pltpu.roll(xf, LANES - ROPE_FREQS, 1),
                        pltpu.roll(xf, ROPE_FREQS, 1))
    return xf * cos + partner * sin_signed


def _inproj_kernel(x_ref, mod_ref, g_ref, w_ref, wt_ref, cos_ref, sin_ref, *out_refs,
                   n_rope, q_scale, has_q, chunk):
    xf = x_ref[0]
    shift = mod_ref[0, 0:1, :]
    scale = mod_ref[0, 1:2, :]
    h = (_rms(xf) * g_ref[...]) * (1.0 + scale) + shift
    hb = h.astype(BF16)
    nat_refs, vtd_ref, vtn_ref = out_refs[:-2], out_refs[-2], out_refs[-1]
    col0 = 0
    for oi, o_ref in enumerate(nat_refs):
        width = o_ref.shape[-1]
        is_q = has_q and oi == 0
        roped = has_q and oi < 2
        for c0 in range(0, width, chunk):
            p = _dot(hb, w_ref[:, col0 + c0:col0 + c0 + chunk])
            for c in range(c0 // LANES, (c0 + chunk) // LANES):
                pc = p[:, c * LANES - c0:(c + 1) * LANES - c0]
                if roped and c < n_rope:
                    pc = _rope(pc, cos_ref[...], sin_ref[...])
                if is_q:
                    pc = pc * q_scale
                o_ref[0, :, c * LANES:(c + 1) * LANES] = pc.astype(o_ref.dtype)
        col0 += width
    n_d = vtd_ref.shape[1]
    vtd_ref[0] = _dot_nt(wt_ref[0:n_d, :], hb).astype(vtd_ref.dtype)
    vn = _dot_nt(wt_ref[n_d:, :], hb)
    for hd in range(NA_HEADS):
        r0 = hd * NA_SLAB
        vtn_ref[0, r0:r0 + HEAD_DIM, :] = vn[hd * HEAD_DIM:(hd + 1) * HEAD_DIM].astype(vtn_ref.dtype)
        vtn_ref[0, r0 + HEAD_DIM:r0 + NA_SLAB, :] = jnp.ones((NA_SLAB - HEAD_DIM, vn.shape[1]),
                                                             vtn_ref.dtype)


def _inproj(x, mod, g, w, wt, cos, sin, *, ts, has_q):
    b, t, d = x.shape
    t_widths = [DIFF_HEADS * 2 * HEAD_DIM, NA_HEADS * NA_SLAB]
    nat_widths = [d] * (w.shape[1] // d)
    per_batch_mod = mod.shape[0] > 1
    kern = functools.partial(_inproj_kernel, n_rope=(DIFF_HEADS if has_q else 0),
                             q_scale=HEAD_DIM ** -0.5 * LOG2E, has_q=has_q, chunk=512)
    return pl.pallas_call(
        kern,
        out_shape=[jax.ShapeDtypeStruct((b, t, wd), BF16) for wd in nat_widths]
        + [jax.ShapeDtypeStruct((b, wd, t), BF16) for wd in t_widths],
        grid=(b, t // ts),
        in_specs=[pl.BlockSpec((1, ts, d), lambda i, j: (i, j, 0)),
                  pl.BlockSpec((1,) + mod.shape[1:],
                               (lambda i, j: (i, 0, 0)) if per_batch_mod else (lambda i, j: (0, 0, 0))),
                  _const_spec(g.shape),
                  _const_spec(w.shape),
                  _const_spec(wt.shape),
                  pl.BlockSpec((ts, LANES), lambda i, j: (j, 0)),
                  pl.BlockSpec((ts, LANES), lambda i, j: (j, 0))],
        out_specs=[pl.BlockSpec((1, ts, wd), lambda i, j: (i, j, 0)) for wd in nat_widths]
        + [pl.BlockSpec((1, wd, ts), lambda i, j: (i, 0, j)) for wd in t_widths],
        compiler_params=_params("parallel", "arbitrary"),
        name="inproj_q" if has_q else "inproj_ctx",
    )(x, mod, g, w, wt, cos, sin)


def _ds(start, size):
    if isinstance(start, int):
        return pl.ds(start, size)
    return pl.ds(pl.multiple_of(start, size), size)


def _run_pipeline(n, stages, n_sets, issue_order=None, static_edge=0):
    k = len(stages)
    issue_order = tuple(range(k)) if issue_order is None else issue_order
    assert sorted(issue_order) == list(range(k))
    assert n % n_sets == 0 and n >= n_sets + k + 2 * static_edge

    def step(t, u):
        for i in issue_order:
            c = t - i
            if isinstance(c, int) and not 0 <= c < n:
                continue
            stages[i](c, (u - i) % n_sets)

    t0 = -(-(static_edge + k - 1) // n_sets) * n_sets
    t1 = t0 + (n - static_edge - t0) // n_sets * n_sets
    for t in range(t0):
        step(t, t % n_sets)

    def body(tt, carry):
        for u in range(n_sets):
            step(tt * n_sets + u, u)
        return carry

    lax.fori_loop(t0 // n_sets, t1 // n_sets, body, 0)
    for t in range(t1, n + k - 1):
        step(t, t % n_sets)


def _diff_kernel(q_ref, k_ref, vt_ref, kc_ref, vct_ref, lamv_ref, g_ref, o_ref,
                 kall, vtall, eb_a, eb_b, *hand_off, tq):
    sets = tuple(tuple(hand_off[i:i + 3]) for i in range(0, len(hand_off), 3))
    ebs = (eb_a, eb_b)
    s = k_ref.shape[1]
    w = 2 * HEAD_DIM
    n_j = s // tq
    for h in range(DIFF_HEADS):
        cols = slice(h * w, (h + 1) * w)
        kall[h, 0:s, :] = k_ref[0, :, cols]
        kall[h, s:, :] = kc_ref[0, :, cols]
        vtall[h, 0:w, 0:s] = vt_ref[0, cols, :]
        vtall[h, 0:w, s:] = vct_ref[0, cols, :]
        vtall[h, w:, :] = jnp.ones((vtall.shape[1] - w, vtall.shape[2]), vtall.dtype)
    lv = lamv_ref[...]
    lam = (jnp.exp(jnp.sum(lv[0:1] * lv[1:2], axis=-1, keepdims=True))
           - jnp.exp(jnp.sum(lv[2:3] * lv[3:4], axis=-1, keepdims=True)) + LAM_INIT)

    def chunk(c):
        h, j = c // n_j, c % n_j
        return h, (_ds(j * tq, tq), _ds(h * w, w))

    def scores(c, p):
        st, mb, _ = sets[p]
        h, idx = chunk(c)
        q = q_ref[(0,) + idx]
        lane = lax.broadcasted_iota(jnp.int32, q.shape, 1)
        for i in range(2):
            sel = (lane < HEAD_DIM) if i == 0 else (lane >= HEAD_DIM)
            qi = jnp.where(sel, q, jnp.zeros_like(q))
            x = _dot_nt(kall[h], qi)
            st[i] = x
            mb[i] = jnp.max(x, axis=0, keepdims=True)

    def values(c, p):
        st, mb, ot = sets[p]
        h, _ = chunk(c)
        for i in range(2):
            e = jnp.exp2(st[i] - mb[i]).astype(BF16)
            ot[i] = _dot(vtall[h], e)

    def finish(c, p):
        _, _, ot = sets[p]
        _, idx = chunk(c)
        o1 = ot[0, 0:w, :] / ot[0, w:w + 1, :]
        o2 = ot[1, 0:w, :] / ot[1, w:w + 1, :]
        o = (o1 - lam * o2).T
        o_ref[(0,) + idx] = (_rms(o) * g_ref[...] * (1.0 - LAM_INIT)).astype(o_ref.dtype)

    def scores_unshifted(c, p):
        h, idx = chunk(c)
        q = q_ref[(0,) + idx]
        lane = lax.broadcasted_iota(jnp.int32, q.shape, 1)
        for i in range(2):
            sel = (lane < HEAD_DIM) if i == 0 else (lane >= HEAD_DIM)
            qi = jnp.where(sel, q, jnp.zeros_like(q))
            ebs[p][i] = jnp.exp2(_dot_nt(kall[h], qi)).astype(BF16)

    def values_unshifted(c, p):
        ot = sets[p][2]
        h, _ = chunk(c)
        for i in range(2):
            ot[i] = _dot(vtall[h], ebs[p][i])

    d_half = q_ref.shape[2]
    grp = (lax.broadcasted_iota(jnp.int32, (d_half, LANES), 0) // HEAD_DIM
           == lax.broadcasted_iota(jnp.int32, (d_half, LANES), 1)).astype(BF16)

    def max_norm2(x):
        xf = x.astype(F32)
        return jnp.max(_dot((xf * xf).astype(BF16), grp))

    k_norm2 = jnp.maximum(max_norm2(k_ref[0]), max_norm2(kc_ref[0]))
    bounded = max_norm2(q_ref[0]) * k_norm2 * SCORE_BOUND_MARGIN < MAX_UNSHIFTED_SCORE ** 2

    n_items = DIFF_HEADS * n_j
    lax.cond(bounded,
             lambda: _run_pipeline(n_items, (scores_unshifted, values_unshifted, finish), len(sets)),
             lambda: _run_pipeline(n_items, (scores, values, finish), len(sets), DIFF_ORDER))


def _diff_attention(q, k, vt, kc, vct, lamv, g, *, tq):
    b, s, d = q.shape
    c = kc.shape[1]
    w = 2 * HEAD_DIM
    half = DIFF_HEADS * w
    ones_rows = 2 * SUBLANES
    hand_off = [pltpu.VMEM((2, s + c, tq), F32), pltpu.VMEM((2, 1, tq), F32),
                pltpu.VMEM((2, w + ones_rows, tq), F32)]
    return pl.pallas_call(
        functools.partial(_diff_kernel, tq=tq),
        out_shape=jax.ShapeDtypeStruct((b, s, half), BF16),
        grid=(b,),
        in_specs=[pl.BlockSpec((1, s, half), lambda i: (i, 0, 0)),
                  pl.BlockSpec((1, s, half), lambda i: (i, 0, 0)),
                  pl.BlockSpec((1, half, s), lambda i: (i, 0, 0)),
                  pl.BlockSpec((1, c, half), lambda i: (i, 0, 0)),
                  pl.BlockSpec((1, half, c), lambda i: (i, 0, 0)),
                  pl.BlockSpec(lamv.shape, lambda i: (0, 0)),
                  pl.BlockSpec(g.shape, lambda i: (0, 0))],
        out_specs=pl.BlockSpec((1, s, half), lambda i: (i, 0, 0)),
        scratch_shapes=[pltpu.VMEM((DIFF_HEADS, s + c, w), BF16),
                        pltpu.VMEM((DIFF_HEADS, w + ones_rows, s + c), BF16),
                        pltpu.VMEM((2, s + c, tq), BF16),
                        pltpu.VMEM((2, s + c, tq), BF16)] + hand_off * 2,
        compiler_params=_params("parallel"),
        name="diff_attn",
    )(q, k, vt, kc, vct, lamv, g)


def _natten_kernel(q_ref, k_ref, vt_ref, kc_ref, vct_ref, toep_ref, o_ref,
                   bias_tab, on_buf, *hand_off, rows, cases, case_ids):
    sets = tuple(tuple(hand_off[i:i + 4]) for i in range(0, len(hand_off), 4))
    n_lat = NA_KROWS * GRID_W
    n_q = NA_QROWS * GRID_W
    n_pairs = NA_QROWS // 2
    live = [[[x for x in range(NA_KROWS)
              if max(pat[x * NA_QROWS + 2 * rp], pat[x * NA_QROWS + 2 * rp + 1]) >= 0]
             for rp in range(n_pairs)] for pat in cases]
    common = max(set(case_ids), key=case_ids.count)
    edge_blocks = max(case_ids.index(common), case_ids[::-1].index(common))
    assert all(ci == common for ci in case_ids[edge_blocks:len(case_ids) - edge_blocks])

    def case_of(c):
        return case_ids[c // NA_HEADS] if isinstance(c, int) else common

    def block(x, rp):
        return slice(x * GRID_W, (x + 1) * GRID_W), slice(rp * LANES, (rp + 1) * LANES)

    @pl.when(pl.program_id(0) == 0)
    def _():
        lane = lax.broadcasted_iota(jnp.int32, (GRID_W, LANES), 1)
        masked = jnp.full((GRID_W, LANES), NEG_INF, F32)
        for ci, pat in enumerate(cases):
            for x in range(NA_KROWS):
                for rp in range(NA_QROWS // 2):
                    d0, d1 = pat[x * NA_QROWS + 2 * rp], pat[x * NA_QROWS + 2 * rp + 1]
                    for h in range(NA_HEADS):
                        left = toep_ref[h, d0] if d0 >= 0 else masked
                        right = toep_ref[h, d1] if d1 >= 0 else masked
                        blk = masked if max(d0, d1) < 0 else jnp.where(lane < GRID_W, left, right)
                        bias_tab[ci, h, x * GRID_W:(x + 1) * GRID_W, rp * LANES:(rp + 1) * LANES] = blk

    def item(c):
        rb, h = c // NA_HEADS, c % NA_HEADS
        ustart = rb * NA_QROWS - NA_KH // 2
        if isinstance(c, int):
            k_rows = pl.ds(min(max(ustart, 0), rows - NA_KROWS) * GRID_W, n_lat)
            slab = pl.ds(h * NA_SLAB, NA_SLAB)
        else:
            k_rows = pl.ds(pl.multiple_of(jnp.clip(ustart, 0, rows - NA_KROWS) * GRID_W, n_q), n_lat)
            slab = pl.ds(pl.multiple_of(h * NA_SLAB, 2 * SUBLANES), NA_SLAB)
        return rb, h, k_rows, _ds(rb * n_q, n_q), _ds((h // 2) * LANES, LANES), slab

    def scores(c, p):
        st = sets[p][0]
        rb, h, k_rows, q_rows, pair_cols, _ = item(c)
        qp = q_ref[0, q_rows, pair_cols]
        lane = lax.broadcasted_iota(jnp.int32, qp.shape, 1)
        sel = (lane < HEAD_DIM) if p % 2 == 0 else (lane >= HEAD_DIM)
        qh = jnp.where(sel, qp, jnp.zeros_like(qp))
        x_lat = _dot_nt(k_ref[0, k_rows, pair_cols], qh)
        ci = case_of(c)
        for rp in range(n_pairs):
            for x in live[ci][rp]:
                bk = block(x, rp)
                st[bk] = x_lat[bk] + bias_tab[(ci, h) + bk]
        st[n_lat:, :] = _dot_nt(kc_ref[0, :, pair_cols], qh)

    def col_max(c, p):
        st, mb, _, _ = sets[p]
        ci = case_of(c)
        for rp in range(n_pairs):
            lanes = block(0, rp)[1]
            m = jnp.max(st[n_lat:, lanes], axis=0, keepdims=True)
            for x in live[ci][rp]:
                m = jnp.maximum(m, jnp.max(st[block(x, rp)], axis=0, keepdims=True))
            mb[:, lanes] = m

    def exps(c, p):
        st, mb, eb, _ = sets[p]
        ci = case_of(c)
        for rp in range(n_pairs):
            lanes = block(0, rp)[1]
            m = mb[:, lanes]
            for x in range(NA_KROWS):
                bk = block(x, rp)
                if x in live[ci][rp]:
                    eb[bk] = jnp.exp2((st[bk] - m).astype(BF16))
                else:
                    eb[bk] = jnp.exp2(jnp.full((GRID_W, LANES), NEG_INF, F32) - m).astype(BF16)
            eb[n_lat:, lanes] = jnp.exp2((st[n_lat:, lanes] - m).astype(BF16))

    def values(c, p):
        _, _, eb, ot = sets[p]
        _, _, k_rows, _, _, slab = item(c)
        ot[...] = (_dot(vt_ref[0, slab, k_rows], eb[0:n_lat, :])
                   + _dot(vct_ref[0, slab, :], eb[n_lat:, :]))

    def finish(c, p):
        _, _, _, ot = sets[p]
        _, _, _, q_rows, pair_cols, _ = item(c)
        half = p % 2
        on_buf[half * HEAD_DIM:(half + 1) * HEAD_DIM, :] = (
            ot[0:HEAD_DIM, :] / ot[HEAD_DIM:HEAD_DIM + 1, :])
        if half == 1:
            o_ref[0, q_rows, pair_cols] = on_buf[...].T.astype(o_ref.dtype)

    def scores_unshifted(c, p):
        eb = sets[p][2]
        rb, h, k_rows, q_rows, pair_cols, _ = item(c)
        qp = q_ref[0, q_rows, pair_cols]
        lane = lax.broadcasted_iota(jnp.int32, qp.shape, 1)
        sel = (lane < HEAD_DIM) if p % 2 == 0 else (lane >= HEAD_DIM)
        qh = jnp.where(sel, qp, jnp.zeros_like(qp))
        x_lat = _dot_nt(k_ref[0, k_rows, pair_cols], qh)
        ci = case_of(c)
        for rp in range(n_pairs):
            for x in range(NA_KROWS):
                bk = block(x, rp)
                if x in live[ci][rp]:
                    eb[bk] = jnp.exp2(x_lat[bk] + bias_tab[(ci, h) + bk]).astype(BF16)
                else:
                    eb[bk] = (x_lat[bk] * 0.0).astype(BF16)
        eb[n_lat:, :] = jnp.exp2(_dot_nt(kc_ref[0, :, pair_cols], qh)).astype(BF16)

    w_na = q_ref.shape[2]
    grp = (lax.broadcasted_iota(jnp.int32, (w_na, LANES), 0) // HEAD_DIM
           == lax.broadcasted_iota(jnp.int32, (w_na, LANES), 1)).astype(BF16)

    def max_norm2(x):
        xf = x.astype(F32)
        return jnp.max(_dot((xf * xf).astype(BF16), grp))

    tv = toep_ref[...]
    bias_max = jnp.max(jnp.where(tv > 0.5 * NEG_INF, jnp.abs(tv), 0.0))
    k_norm2 = jnp.maximum(max_norm2(k_ref[0]), max_norm2(kc_ref[0]))
    qk_max = jnp.sqrt(max_norm2(q_ref[0]) * k_norm2) * SCORE_BOUND_MARGIN
    bounded = qk_max + bias_max < MAX_UNSHIFTED_SCORE

    assert len(sets) % 2 == 0
    n_items = (rows // NA_QROWS) * NA_HEADS
    edge = edge_blocks * NA_HEADS
    lax.cond(bounded,
             lambda: _run_pipeline(n_items, (scores_unshifted, values, finish), len(sets),
                                   static_edge=edge),
             lambda: _run_pipeline(n_items, (scores, col_max, exps, values, finish), len(sets),
                                   NA_ORDER, static_edge=edge))


def _na_block_patterns(rows):
    pats = []
    for rb in range(rows // NA_QROWS):
        ustart = int(np.clip(rb * NA_QROWS - NA_KH // 2, 0, rows - NA_KROWS))
        pat = []
        for x in range(NA_KROWS):
            for rr in range(NA_QROWS):
                r = rb * NA_QROWS + rr
                start = int(np.clip(r - NA_KH // 2, 0, rows - NA_KH))
                a = ustart + x
                pat.append(a - r + NA_KH - 1 if start <= a < start + NA_KH else -1)
        pats.append(tuple(pat))
    return pats


def _na_block_cases(rows):
    pats = _na_block_patterns(rows)
    cases = tuple(sorted(set(pats), key=pats.index))
    return cases, tuple(cases.index(p) for p in pats)


def _na_toeplitz(rpb):
    jq = np.arange(GRID_W)
    col_start = np.clip(jq - NA_KW // 2, 0, GRID_W - NA_KW)
    in_win = (jq[None, :] >= col_start[:, None]) & (jq[None, :] < col_start[:, None] + NA_KW)
    dc_idx = np.clip(jq[None, :] - jq[:, None] + NA_KW - 1, 0, 2 * NA_KW - 2)
    onehot = (dc_idx.T[None] == np.arange(2 * NA_KW - 1)[:, None, None]) & in_win.T[None]
    toep = jnp.einsum("hdj,jkq->hdkq", rpb, jnp.asarray(onehot, F32),
                      precision=lax.Precision.HIGHEST)
    toep = jnp.where(jnp.asarray(in_win.T)[None, None], toep * LOG2E, NEG_INF)
    return jnp.concatenate([toep, toep], axis=-1)


def _natten(q, k, vt, kc, vct, toep, cases, case_ids):
    b, s, d = q.shape
    c = kc.shape[1]
    rows = s // GRID_W
    w = NA_HEADS * HEAD_DIM
    n_q = NA_QROWS * GRID_W
    n_keys = NA_KROWS * GRID_W + c
    hand_off = [pltpu.VMEM((n_keys, n_q), F32), pltpu.VMEM((1, n_q), F32),
                pltpu.VMEM((n_keys, n_q), BF16),
                pltpu.VMEM((NA_SLAB, n_q), F32)]
    return pl.pallas_call(
        functools.partial(_natten_kernel, rows=rows, cases=cases, case_ids=case_ids),
        out_shape=jax.ShapeDtypeStruct((b, s, w), BF16),
        grid=(b,),
        in_specs=[pl.BlockSpec((1, s, w), lambda i: (i, 0, 1)),
                  pl.BlockSpec((1, s, w), lambda i: (i, 0, 1)),
                  pl.BlockSpec((1,) + vt.shape[1:], lambda i: (i, 0, 0)),
                  pl.BlockSpec((1, c, w), lambda i: (i, 0, 1)),
                  pl.BlockSpec((1,) + vct.shape[1:], lambda i: (i, 0, 0)),
                  _const_spec(toep.shape)],
        out_specs=pl.BlockSpec((1, s, w), lambda i: (i, 0, 0)),
        scratch_shapes=[pltpu.VMEM((len(cases), NA_HEADS, NA_KROWS * GRID_W, n_q), F32),
                        pltpu.VMEM((LANES, n_q), F32)] + hand_off * NA_SETS,
        compiler_params=_params("arbitrary"),
        name="natten",
    )(q, k, vt, kc, vct, toep)


def _tail_kernel(x_ref, od_ref, on_ref, mod_ref, wo_ref, g2_ref, w1_ref, w2_ref, gf_ref, o_ref,
                 *, ff_chunk):
    half = od_ref.shape[-1]
    gate_a = mod_ref[0, 2:3, :]
    shift_m = mod_ref[0, 3:4, :]
    scale_m = mod_ref[0, 4:5, :]
    gate_m = mod_ref[0, 5:6, :]
    attn = _dot(od_ref[0], wo_ref[:half, :]) + _dot(on_ref[0], wo_ref[half:, :])
    x1 = x_ref[0] + gate_a * attn
    h2 = ((_rms(x1) * g2_ref[...]) * (1.0 + scale_m) + shift_m).astype(BF16)
    y = None
    for c in range(w1_ref.shape[1] // ff_chunk):
        cols = slice(c * ff_chunk, (c + 1) * ff_chunk)
        a = jnp.maximum(_dot(h2, w1_ref[:, cols]), 0.0)
        part = _dot((a * a).astype(BF16), w2_ref[cols, :])
        y = part if y is None else y + part
    x2 = x1 + gate_m * y
    o_ref[0] = _rms(x2) * gf_ref[...]


def _tail(x, o_d, o_n, mod, w_out, g2, w1, w2, gf, *, ts, ff_chunk=1024):
    b, s, d = x.shape
    half = o_d.shape[-1]
    return pl.pallas_call(
        functools.partial(_tail_kernel, ff_chunk=ff_chunk),
        out_shape=jax.ShapeDtypeStruct((b, s, d), F32),
        grid=(b, s // ts),
        in_specs=[pl.BlockSpec((1, ts, d), lambda i, j: (i, j, 0)),
                  pl.BlockSpec((1, ts, half), lambda i, j: (i, j, 0)),
                  pl.BlockSpec((1, ts, half), lambda i, j: (i, j, 0)),
                  pl.BlockSpec((1,) + mod.shape[1:], lambda i, j: (i, 0, 0)),
                  _const_spec(w_out.shape),
                  _const_spec(g2.shape),
                  _const_spec(w1.shape),
                  _const_spec(w2.shape),
                  _const_spec(gf.shape)],
        out_specs=pl.BlockSpec((1, ts, d), lambda i, j: (i, j, 0)),
        compiler_params=_params("parallel", "arbitrary"),
        name="tail",
    )(x, o_d, o_n, mod, w_out, g2, w1, w2, gf)


def _rope_tables(s):
    pos = np.arange(s)
    inv = ROPE_BASE ** (-np.arange(ROPE_FREQS, dtype=np.float32) / ROPE_FREQS)
    ang_r = (pos // GRID_W).astype(np.float32)[:, None] * inv
    ang_c = (pos % GRID_W).astype(np.float32)[:, None] * inv
    cos = np.concatenate([np.cos(ang_r)] * 2 + [np.cos(ang_c)] * 2, axis=1)
    sin = np.concatenate([-np.sin(ang_r), np.sin(ang_r), -np.sin(ang_c), np.sin(ang_c)], axis=1)
    reps = LANES // HEAD_DIM
    return (jnp.asarray(np.tile(cos, (1, reps)), F32), jnp.asarray(np.tile(sin, (1, reps)), F32))


def _tiles(s):
    return min(1024, s), min(512, s), min(256, s)


def kernel(x, c, ctx, c_ctx, w_mod, b_mod, norm1_g, w_in, lam_q1, lam_k1, lam_q2, lam_k2,
           diff_subln_g, na_rpb, w_out, norm2_g, w_fc1, w_fc2, final_g):
    b, s, d = x.shape
    rows = s // GRID_W
    assert w_mod.shape[0] == 1, "single-layer block"
    assert s % (GRID_W * NA_QROWS) == 0 and rows >= NA_KROWS
    ts_proj, ts_tail, tq = _tiles(s)

    n_cond = b + 1
    pad = (-n_cond) % SUBLANES
    cond = jnp.concatenate([c, c_ctx[None, :], jnp.zeros((pad, d), F32)], axis=0)
    mod = _mod(cond, w_mod[0], b_mod[0][None, :])
    mod_x = mod[:b].reshape(b, 6, d)
    mod_c = mod[b:b + 1].reshape(1, 6, d)

    w_in_b = w_in[0].astype(BF16)
    w_qk, w_v_t = w_in_b[:, :2 * d], w_in_b[:, 2 * d:].T
    cos, sin = _rope_tables(s)
    g1 = norm1_g[0][None, :]
    q, k, vt_d, vt_n = _inproj(x, mod_x, g1, w_qk, w_v_t, cos, sin, ts=ts_proj, has_q=True)
    n_ctx = ctx.shape[1]
    kc, vct_d, vct_n = _inproj(ctx, mod_c, g1, w_qk[:, d:], w_v_t, cos[:n_ctx], sin[:n_ctx],
                               ts=n_ctx, has_q=False)

    lamv = jnp.stack([lam_q1[0], lam_k1[0], lam_q2[0], lam_k2[0]])
    o_d = _diff_attention(q, k, vt_d, kc, vct_d, lamv, diff_subln_g[0][None, :], tq=tq)
    cases, case_ids = _na_block_cases(rows)
    o_n = _natten(q, k, vt_n, kc, vct_n, _na_toeplitz(na_rpb[0]), cases, case_ids)

    return _tail(x, o_d, o_n, mod_x, w_out[0].astype(BF16), norm2_g[0][None, :],
                 w_fc1[0].astype(BF16), w_fc2[0].astype(BF16), final_g[None, :], ts=ts_tail)
```

```python
import functools
import math

import numpy as np
import jax
import jax.numpy as jnp
from jax import lax
from jax.experimental import pallas as pl
from jax.experimental.pallas import tpu as pltpu

GRID_W = 64
HEAD_DIM = 64
DIFF_HEADS = 4
NA_HEADS = 8
NA_KH = 8
NA_KW = 16
ROPE_BASE = 10000.0
ROPE_FREQS = HEAD_DIM // 4
EPS = 1e-6
NEG_INF = -1e30
LAM_INIT = 0.8 - 0.6 * math.exp(-0.3 * 0)
LOG2E = math.log2(math.e)

LANES = 128
SUBLANES = 8
VMEM_LIMIT = 56 * 1024 * 1024

NA_QROWS = 4
NA_KROWS = 12
NA_SLAB = HEAD_DIM + 16
NA_SETS = 4
DIFF_ORDER = (0, 1, 2)
MAX_UNSHIFTED_SCORE = 60.0
SCORE_BOUND_MARGIN = 1.05
NA_ORDER = (3, 0, 1, 2, 4)

BF16 = jnp.bfloat16
F32 = jnp.float32


def _params(*sem):
    return pltpu.CompilerParams(dimension_semantics=sem, vmem_limit_bytes=VMEM_LIMIT)


def _const_spec(shape):
    nd = len(shape)
    return pl.BlockSpec(shape, lambda *_: (0,) * nd, pipeline_mode=pl.Buffered(1))


def _dot(a, b):
    return jnp.dot(a, b, preferred_element_type=F32)


def _dot_nt(a, b):
    return lax.dot_general(a, b, (((1,), (1,)), ((), ())), preferred_element_type=F32)


def _rms(xf):
    return xf * lax.rsqrt(jnp.mean(xf * xf, axis=-1, keepdims=True) + EPS)


def _mod_kernel(cond_ref, w_ref, b_ref, o_ref):
    cnd = cond_ref[...]
    act = (cnd * jax.nn.sigmoid(cnd)).astype(BF16)
    o_ref[...] = _dot(act, w_ref[...].astype(BF16)) + b_ref[...]


def _mod(cond, w_mod, b_mod, tn=768):
    r, d = cond.shape
    n = w_mod.shape[1]
    return pl.pallas_call(
        _mod_kernel,
        out_shape=jax.ShapeDtypeStruct((r, n), F32),
        grid=(n // tn,),
        in_specs=[pl.BlockSpec((r, d), lambda j: (0, 0)),
                  pl.BlockSpec((d, tn), lambda j: (0, j)),
                  pl.BlockSpec((1, tn), lambda j: (0, j))],
        out_specs=pl.BlockSpec((r, tn), lambda j: (0, j)),
        compiler_params=_params("arbitrary"),
        name="mod",
    )(cond, w_mod, b_mod)


def _rope(xf, cos, sin_signed):
    lane = lax.broadcasted_iota(jnp.int32, xf.shape, 1)
    first = (lane % (2 * ROPE_FREQS)) < ROPE_FREQS
    partner = jnp.where(first,
                        pltpu.roll(xf, LANES - ROPE_FREQS, 1),
                        pltpu.roll(xf, ROPE_FREQS, 1))
    return xf * cos + partner * sin_signed


def _inproj_kernel(x_ref, mod_ref, g_ref, w_ref, wt_ref, cos_ref, sin_ref, *out_refs,
                   n_rope, q_scale, has_q, chunk):
    xf = x_ref[0]
    shift = mod_ref[0, 0:1, :]
    scale = mod_ref[0, 1:2, :]
    h = (_rms(xf) * g_ref[...]) * (1.0 + scale) + shift
    hb = h.astype(BF16)
    nat_refs, vtd_ref, vtn_ref = out_refs[:-2], out_refs[-2], out_refs[-1]
    col0 = 0
    for oi, o_ref in enumerate(nat_refs):
        width = o_ref.shape[-1]
        is_q = has_q and oi == 0
        roped = has_q and oi < 2
        for c0 in range(0, width, chunk):
            p = _dot(hb, w_ref[:, col0 + c0:col0 + c0 + chunk])
            for c in range(c0 // LANES, (c0 + chunk) // LANES):
                pc = p[:, c * LANES - c0:(c + 1) * LANES - c0]
                if roped and c < n_rope:
                    pc = _rope(pc, cos_ref[...], sin_ref[...])
                if is_q:
                    pc = pc * q_scale
                o_ref[0, :, c * LANES:(c + 1) * LANES] = pc.astype(o_ref.dtype)
        col0 += width
    n_d = vtd_ref.shape[1]
    vtd_ref[0] = _dot_nt(wt_ref[0:n_d, :], hb).astype(vtd_ref.dtype)
    vn = _dot_nt(wt_ref[n_d:, :], hb)
    for hd in range(NA_HEADS):
        r0 = hd * NA_SLAB
        vtn_ref[0, r0:r0 + HEAD_DIM, :] = vn[hd * HEAD_DIM:(hd + 1) * HEAD_DIM].astype(vtn_ref.dtype)
        vtn_ref[0, r0 + HEAD_DIM:r0 + NA_SLAB, :] = jnp.ones((NA_SLAB - HEAD_DIM, vn.shape[1]),
                                                             vtn_ref.dtype)


def _inproj(x, mod, g, w, wt, cos, sin, *, ts, has_q):
    b, t, d = x.shape
    t_widths = [DIFF_HEADS * 2 * HEAD_DIM, NA_HEADS * NA_SLAB]
    nat_widths = [d] * (w.shape[1] // d)
    per_batch_mod = mod.shape[0] > 1
    kern = functools.partial(_inproj_kernel, n_rope=(DIFF_HEADS if has_q else 0),
                             q_scale=HEAD_DIM ** -0.5 * LOG2E, has_q=has_q, chunk=512)
    return pl.pallas_call(
        kern,
        out_shape=[jax.ShapeDtypeStruct((b, t, wd), BF16) for wd in nat_widths]
        + [jax.ShapeDtypeStruct((b, wd, t), BF16) for wd in t_widths],
        grid=(b, t // ts),
        in_specs=[pl.BlockSpec((1, ts, d), lambda i, j: (i, j, 0)),
                  pl.BlockSpec((1,) + mod.shape[1:],
                               (lambda i, j: (i, 0, 0)) if per_batch_mod else (lambda i, j: (0, 0, 0))),
                  _const_spec(g.shape),
                  _const_spec(w.shape),
                  _const_spec(wt.shape),
                  pl.BlockSpec((ts, LANES), lambda i, j: (j, 0)),
                  pl.BlockSpec((ts, LANES), lambda i, j: (j, 0))],
        out_specs=[pl.BlockSpec((1, ts, wd), lambda i, j: (i, j, 0)) for wd in nat_widths]
        + [pl.BlockSpec((1, wd, ts), lambda i, j: (i, 0, j)) for wd in t_widths],
        compiler_params=_params("parallel", "arbitrary"),
        name="inproj_q" if has_q else "inproj_ctx",
    )(x, mod, g, w, wt, cos, sin)


def _ds(start, size):
    if isinstance(start, int):
        return pl.ds(start, size)
    return pl.ds(pl.multiple_of(start, size), size)


def _run_pipeline(n, stages, n_sets, issue_order=None, static_edge=0):
    k = len(stages)
    issue_order = tuple(range(k)) if issue_order is None else issue_order
    assert sorted(issue_order) == list(range(k))
    assert n % n_sets == 0 and n >= n_sets + k + 2 * static_edge

    def step(t, u):
        for i in issue_order:
            c = t - i
            if isinstance(c, int) and not 0 <= c < n:
                continue
            stages[i](c, (u - i) % n_sets)

    t0 = -(-(static_edge + k - 1) // n_sets) * n_sets
    t1 = t0 + (n - static_edge - t0) // n_sets * n_sets
    for t in range(t0):
        step(t, t % n_sets)

    def body(tt, carry):
        for u in range(n_sets):
            step(tt * n_sets + u, u)
        return carry

    lax.fori_loop(t0 // n_sets, t1 // n_sets, body, 0)
    for t in range(t1, n + k - 1):
        step(t, t % n_sets)


def _diff_kernel(q_ref, k_ref, vt_ref, kc_ref, vct_ref, lamv_ref, g_ref, o_ref,
                 kall, vtall, eb_a, eb_b, *hand_off, tq):
    sets = tuple(tuple(hand_off[i:i + 3]) for i in range(0, len(hand_off), 3))
    ebs = (eb_a, eb_b)
    s = k_ref.shape[1]
    w = 2 * HEAD_DIM
    n_j = s // tq
    for h in range(DIFF_HEADS):
        cols = slice(h * w, (h + 1) * w)
        kall[h, 0:s, :] = k_ref[0, :, cols]
        kall[h, s:, :] = kc_ref[0, :, cols]
        vtall[h, 0:w, 0:s] = vt_ref[0, cols, :]
        vtall[h, 0:w, s:] = vct_ref[0, cols, :]
        vtall[h, w:, :] = jnp.ones((vtall.shape[1] - w, vtall.shape[2]), vtall.dtype)
    lv = lamv_ref[...]
    lam = (jnp.exp(jnp.sum(lv[0:1] * lv[1:2], axis=-1, keepdims=True))
           - jnp.exp(jnp.sum(lv[2:3] * lv[3:4], axis=-1, keepdims=True)) + LAM_INIT)

    def chunk(c):
        h, j = c // n_j, c % n_j
        return h, (_ds(j * tq, tq), _ds(h * w, w))

    def scores(c, p):
        st, mb, _ = sets[p]
        h, idx = chunk(c)
        q = q_ref[(0,) + idx]
        lane = lax.broadcasted_iota(jnp.int32, q.shape, 1)
        for i in range(2):
            sel = (lane < HEAD_DIM) if i == 0 else (lane >= HEAD_DIM)
            qi = jnp.where(sel, q, jnp.zeros_like(q))
            x = _dot_nt(kall[h], qi)
            st[i] = x
            mb[i] = jnp.max(x, axis=0, keepdims=True)

    def values(c, p):
        st, mb, ot = sets[p]
        h, _ = chunk(c)
        for i in range(2):
            e = jnp.exp2(st[i] - mb[i]).astype(BF16)
            ot[i] = _dot(vtall[h], e)

    def finish(c, p):
        _, _, ot = sets[p]
        _, idx = chunk(c)
        o1 = ot[0, 0:w, :] / ot[0, w:w + 1, :]
        o2 = ot[1, 0:w, :] / ot[1, w:w + 1, :]
        o = (o1 - lam * o2).T
        o_ref[(0,) + idx] = (_rms(o) * g_ref[...] * (1.0 - LAM_INIT)).astype(o_ref.dtype)

    def scores_unshifted(c, p):
        h, idx = chunk(c)
        q = q_ref[(0,) + idx]
        lane = lax.broadcasted_iota(jnp.int32, q.shape, 1)
        for i in range(2):
            sel = (lane < HEAD_DIM) if i == 0 else (lane >= HEAD_DIM)
            qi = jnp.where(sel, q, jnp.zeros_like(q))
            ebs[p][i] = jnp.exp2(_dot_nt(kall[h], qi)).astype(BF16)

    def values_unshifted(c, p):
        ot = sets[p][2]
        h, _ = chunk(c)
        for i in range(2):
            ot[i] = _dot(vtall[h], ebs[p][i])

    d_half = q_ref.shape[2]
    grp = (lax.broadcasted_iota(jnp.int32, (d_half, LANES), 0) // HEAD_DIM
           == lax.broadcasted_iota(jnp.int32, (d_half, LANES), 1)).astype(BF16)

    def max_norm2(x):
        return jnp.max(_dot(x * x, grp))

    k_norm2 = jnp.maximum(max_norm2(k_ref[0]), max_norm2(kc_ref[0]))
    bounded = max_norm2(q_ref[0]) * k_norm2 * SCORE_BOUND_MARGIN < MAX_UNSHIFTED_SCORE ** 2

    n_items = DIFF_HEADS * n_j
    lax.cond(bounded,
             lambda: _run_pipeline(n_items, (scores_unshifted, values_unshifted, finish), len(sets)),
             lambda: _run_pipeline(n_items, (scores, values, finish), len(sets), DIFF_ORDER))


def _diff_attention(q, k, vt, kc, vct, lamv, g, *, tq):
    b, s, d = q.shape
    c = kc.shape[1]
    w = 2 * HEAD_DIM
    half = DIFF_HEADS * w
    ones_rows = 2 * SUBLANES
    hand_off = [pltpu.VMEM((2, s + c, tq), F32), pltpu.VMEM((2, 1, tq), F32),
                pltpu.VMEM((2, w + ones_rows, tq), F32)]
    return pl.pallas_call(
        functools.partial(_diff_kernel, tq=tq),
        out_shape=jax.ShapeDtypeStruct((b, s, half), BF16),
        grid=(b,),
        in_specs=[pl.BlockSpec((1, s, half), lambda i: (i, 0, 0)),
                  pl.BlockSpec((1, s, half), lambda i: (i, 0, 0)),
                  pl.BlockSpec((1, half, s), lambda i: (i, 0, 0)),
                  pl.BlockSpec((1, c, half), lambda i: (i, 0, 0)),
                  pl.BlockSpec((1, half, c), lambda i: (i, 0, 0)),
                  pl.BlockSpec(lamv.shape, lambda i: (0, 0)),
                  pl.BlockSpec(g.shape, lambda i: (0, 0))],
        out_specs=pl.BlockSpec((1, s, half), lambda i: (i, 0, 0)),
        scratch_shapes=[pltpu.VMEM((DIFF_HEADS, s + c, w), BF16),
                        pltpu.VMEM((DIFF_HEADS, w + ones_rows, s + c), BF16),
                        pltpu.VMEM((2, s + c, tq), BF16),
                        pltpu.VMEM((2, s + c, tq), BF16)] + hand_off * 2,
        compiler_params=_params("parallel"),
        name="diff_attn",
    )(q, k, vt, kc, vct, lamv, g)


def _natten_kernel(q_ref, k_ref, vt_ref, kc_ref, vct_ref, toep_ref, o_ref,
                   bias_tab, on_buf, *hand_off, rows, cases, case_ids):
    sets = tuple(tuple(hand_off[i:i + 4]) for i in range(0, len(hand_off), 4))
    n_lat = NA_KROWS * GRID_W
    n_q = NA_QROWS * GRID_W
    n_pairs = NA_QROWS // 2
    live = [[[x for x in range(NA_KROWS)
              if max(pat[x * NA_QROWS + 2 * rp], pat[x * NA_QROWS + 2 * rp + 1]) >= 0]
             for rp in range(n_pairs)] for pat in cases]
    common = max(set(case_ids), key=case_ids.count)
    edge_blocks = max(case_ids.index(common), case_ids[::-1].index(common))
    assert all(ci == common for ci in case_ids[edge_blocks:len(case_ids) - edge_blocks])

    def case_of(c):
        return case_ids[c // NA_HEADS] if isinstance(c, int) else common

    def block(x, rp):
        return slice(x * GRID_W, (x + 1) * GRID_W), slice(rp * LANES, (rp + 1) * LANES)

    @pl.when(pl.program_id(0) == 0)
    def _():
        lane = lax.broadcasted_iota(jnp.int32, (GRID_W, LANES), 1)
        masked = jnp.full((GRID_W, LANES), NEG_INF, F32)
        for ci, pat in enumerate(cases):
            for x in range(NA_KROWS):
                for rp in range(NA_QROWS // 2):
                    d0, d1 = pat[x * NA_QROWS + 2 * rp], pat[x * NA_QROWS + 2 * rp + 1]
                    for h in range(NA_HEADS):
                        left = toep_ref[h, d0] if d0 >= 0 else masked
                        right = toep_ref[h, d1] if d1 >= 0 else masked
                        blk = masked if max(d0, d1) < 0 else jnp.where(lane < GRID_W, left, right)
                        bias_tab[ci, h, x * GRID_W:(x + 1) * GRID_W, rp * LANES:(rp + 1) * LANES] = blk

    def item(c):
        rb, h = c // NA_HEADS, c % NA_HEADS
        ustart = rb * NA_QROWS - NA_KH // 2
        if isinstance(c, int):
            k_rows = pl.ds(min(max(ustart, 0), rows - NA_KROWS) * GRID_W, n_lat)
            slab = pl.ds(h * NA_SLAB, NA_SLAB)
        else:
            k_rows = pl.ds(pl.multiple_of(jnp.clip(ustart, 0, rows - NA_KROWS) * GRID_W, n_q), n_lat)
            slab = pl.ds(pl.multiple_of(h * NA_SLAB, 2 * SUBLANES), NA_SLAB)
        return rb, h, k_rows, _ds(rb * n_q, n_q), _ds((h // 2) * LANES, LANES), slab

    def scores(c, p):
        st = sets[p][0]
        rb, h, k_rows, q_rows, pair_cols, _ = item(c)
        qp = q_ref[0, q_rows, pair_cols]
        lane = lax.broadcasted_iota(jnp.int32, qp.shape, 1)
        sel = (lane < HEAD_DIM) if p % 2 == 0 else (lane >= HEAD_DIM)
        qh = jnp.where(sel, qp, jnp.zeros_like(qp))
        x_lat = _dot_nt(k_ref[0, k_rows, pair_cols], qh)
        ci = case_of(c)
        for rp in range(n_pairs):
            for x in live[ci][rp]:
                bk = block(x, rp)
                st[bk] = x_lat[bk] + bias_tab[(ci, h) + bk]
        st[n_lat:, :] = _dot_nt(kc_ref[0, :, pair_cols], qh)

    def col_max(c, p):
        st, mb, _, _ = sets[p]
        ci = case_of(c)
        for rp in range(n_pairs):
            lanes = block(0, rp)[1]
            m = jnp.max(st[n_lat:, lanes], axis=0, keepdims=True)
            for x in live[ci][rp]:
                m = jnp.maximum(m, jnp.max(st[block(x, rp)], axis=0, keepdims=True))
            mb[:, lanes] = m

    def exps(c, p):
        st, mb, eb, _ = sets[p]
        ci = case_of(c)
        for rp in range(n_pairs):
            lanes = block(0, rp)[1]
            m = mb[:, lanes]
            for x in range(NA_KROWS):
                bk = block(x, rp)
                if x in live[ci][rp]:
                    eb[bk] = jnp.exp2((st[bk] - m).astype(BF16))
                else:
                    eb[bk] = jnp.exp2(jnp.full((GRID_W, LANES), NEG_INF, F32) - m).astype(BF16)
            eb[n_lat:, lanes] = jnp.exp2((st[n_lat:, lanes] - m).astype(BF16))

    def values(c, p):
        _, _, eb, ot = sets[p]
        _, _, k_rows, _, _, slab = item(c)
        ot[...] = (_dot(vt_ref[0, slab, k_rows], eb[0:n_lat, :])
                   + _dot(vct_ref[0, slab, :], eb[n_lat:, :]))

    def finish(c, p):
        _, _, _, ot = sets[p]
        _, _, _, q_rows, pair_cols, _ = item(c)
        half = p % 2
        on_buf[half * HEAD_DIM:(half + 1) * HEAD_DIM, :] = (
            ot[0:HEAD_DIM, :] / ot[HEAD_DIM:HEAD_DIM + 1, :])
        if half == 1:
            o_ref[0, q_rows, pair_cols] = on_buf[...].T.astype(o_ref.dtype)

    def scores_unshifted(c, p):
        eb = sets[p][2]
        rb, h, k_rows, q_rows, pair_cols, _ = item(c)
        qp = q_ref[0, q_rows, pair_cols]
        lane = lax.broadcasted_iota(jnp.int32, qp.shape, 1)
        sel = (lane < HEAD_DIM) if p % 2 == 0 else (lane >= HEAD_DIM)
        qh = jnp.where(sel, qp, jnp.zeros_like(qp))
        x_lat = _dot_nt(k_ref[0, k_rows, pair_cols], qh)
        ci = case_of(c)
        for rp in range(n_pairs):
            for x in range(NA_KROWS):
                bk = block(x, rp)
                if x in live[ci][rp]:
                    eb[bk] = jnp.exp2(x_lat[bk] + bias_tab[(ci, h) + bk]).astype(BF16)
                else:
                    eb[bk] = (x_lat[bk] * 0.0).astype(BF16)
        eb[n_lat:, :] = jnp.exp2(_dot_nt(kc_ref[0, :, pair_cols], qh)).astype(BF16)

    w_na = q_ref.shape[2]
    grp = (lax.broadcasted_iota(jnp.int32, (w_na, LANES), 0) // HEAD_DIM
           == lax.broadcasted_iota(jnp.int32, (w_na, LANES), 1)).astype(BF16)

    def max_norm2(x):
        return jnp.max(_dot(x * x, grp))

    tv = toep_ref[...]
    bias_max = jnp.max(jnp.where(tv > 0.5 * NEG_INF, jnp.abs(tv), 0.0))
    k_norm2 = jnp.maximum(max_norm2(k_ref[0]), max_norm2(kc_ref[0]))
    qk_max = jnp.sqrt(max_norm2(q_ref[0]) * k_norm2) * SCORE_BOUND_MARGIN
    bounded = qk_max + bias_max < MAX_UNSHIFTED_SCORE

    assert len(sets) % 2 == 0
    n_items = (rows // NA_QROWS) * NA_HEADS
    edge = edge_blocks * NA_HEADS
    lax.cond(bounded,
             lambda: _run_pipeline(n_items, (scores_unshifted, values, finish), len(sets),
                                   static_edge=edge),
             lambda: _run_pipeline(n_items, (scores, col_max, exps, values, finish), len(sets),
                                   NA_ORDER, static_edge=edge))


def _na_block_patterns(rows):
    pats = []
    for rb in range(rows // NA_QROWS):
        ustart = int(np.clip(rb * NA_QROWS - NA_KH // 2, 0, rows - NA_KROWS))
        pat = []
        for x in range(NA_KROWS):
            for rr in range(NA_QROWS):
                r = rb * NA_QROWS + rr
                start = int(np.clip(r - NA_KH // 2, 0, rows - NA_KH))
                a = ustart + x
                pat.append(a - r + NA_KH - 1 if start <= a < start + NA_KH else -1)
        pats.append(tuple(pat))
    return pats


def _na_block_cases(rows):
    pats = _na_block_patterns(rows)
    cases = tuple(sorted(set(pats), key=pats.index))
    return cases, tuple(cases.index(p) for p in pats)


def _na_toeplitz(rpb):
    jq = np.arange(GRID_W)
    col_start = np.clip(jq - NA_KW // 2, 0, GRID_W - NA_KW)
    in_win = (jq[None, :] >= col_start[:, None]) & (jq[None, :] < col_start[:, None] + NA_KW)
    dc_idx = np.clip(jq[None, :] - jq[:, None] + NA_KW - 1, 0, 2 * NA_KW - 2)
    onehot = (dc_idx.T[None] == np.arange(2 * NA_KW - 1)[:, None, None]) & in_win.T[None]
    toep = jnp.einsum("hdj,jkq->hdkq", rpb, jnp.asarray(onehot, F32),
                      precision=lax.Precision.HIGHEST)
    toep = jnp.where(jnp.asarray(in_win.T)[None, None], toep * LOG2E, NEG_INF)
    return jnp.concatenate([toep, toep], axis=-1)


def _natten(q, k, vt, kc, vct, toep, cases, case_ids):
    b, s, d = q.shape
    c = kc.shape[1]
    rows = s // GRID_W
    w = NA_HEADS * HEAD_DIM
    n_q = NA_QROWS * GRID_W
    n_keys = NA_KROWS * GRID_W + c
    hand_off = [pltpu.VMEM((n_keys, n_q), F32), pltpu.VMEM((1, n_q), F32),
                pltpu.VMEM((n_keys, n_q), BF16),
                pltpu.VMEM((NA_SLAB, n_q), F32)]
    return pl.pallas_call(
        functools.partial(_natten_kernel, rows=rows, cases=cases, case_ids=case_ids),
        out_shape=jax.ShapeDtypeStruct((b, s, w), BF16),
        grid=(b,),
        in_specs=[pl.BlockSpec((1, s, w), lambda i: (i, 0, 1)),
                  pl.BlockSpec((1, s, w), lambda i: (i, 0, 1)),
                  pl.BlockSpec((1,) + vt.shape[1:], lambda i: (i, 0, 0)),
                  pl.BlockSpec((1, c, w), lambda i: (i, 0, 1)),
                  pl.BlockSpec((1,) + vct.shape[1:], lambda i: (i, 0, 0)),
                  _const_spec(toep.shape)],
        out_specs=pl.BlockSpec((1, s, w), lambda i: (i, 0, 0)),
        scratch_shapes=[pltpu.VMEM((len(cases), NA_HEADS, NA_KROWS * GRID_W, n_q), F32),
                        pltpu.VMEM((LANES, n_q), F32)] + hand_off * NA_SETS,
        compiler_params=_params("arbitrary"),
        name="natten",
    )(q, k, vt, kc, vct, toep)


def _tail_kernel(x_ref, od_ref, on_ref, mod_ref, wo_ref, g2_ref, w1_ref, w2_ref, gf_ref, o_ref,
                 *, ff_chunk):
    half = od_ref.shape[-1]
    gate_a = mod_ref[0, 2:3, :]
    shift_m = mod_ref[0, 3:4, :]
    scale_m = mod_ref[0, 4:5, :]
    gate_m = mod_ref[0, 5:6, :]
    attn = _dot(od_ref[0], wo_ref[:half, :]) + _dot(on_ref[0], wo_ref[half:, :])
    x1 = x_ref[0] + gate_a * attn
    h2 = ((_rms(x1) * g2_ref[...]) * (1.0 + scale_m) + shift_m).astype(BF16)
    y = None
    for c in range(w1_ref.shape[1] // ff_chunk):
        cols = slice(c * ff_chunk, (c + 1) * ff_chunk)
        a = jnp.maximum(_dot(h2, w1_ref[:, cols]), 0.0)
        part = _dot((a * a).astype(BF16), w2_ref[cols, :])
        y = part if y is None else y + part
    x2 = x1 + gate_m * y
    o_ref[0] = _rms(x2) * gf_ref[...]


def _tail(x, o_d, o_n, mod, w_out, g2, w1, w2, gf, *, ts, ff_chunk=1024):
    b, s, d = x.shape
    half = o_d.shape[-1]
    return pl.pallas_call(
        functools.partial(_tail_kernel, ff_chunk=ff_chunk),
        out_shape=jax.ShapeDtypeStruct((b, s, d), F32),
        grid=(b, s // ts),
        in_specs=[pl.BlockSpec((1, ts, d), lambda i, j: (i, j, 0)),
                  pl.BlockSpec((1, ts, half), lambda i, j: (i, j, 0)),
                  pl.BlockSpec((1, ts, half), lambda i, j: (i, j, 0)),
                  pl.BlockSpec((1,) + mod.shape[1:], lambda i, j: (i, 0, 0)),
                  _const_spec(w_out.shape),
                  _const_spec(g2.shape),
                  _const_spec(w1.shape),
                  _const_spec(w2.shape),
                  _const_spec(gf.shape)],
        out_specs=pl.BlockSpec((1, ts, d), lambda i, j: (i, j, 0)),
        compiler_params=_params("parallel", "arbitrary"),
        name="tail",
    )(x, o_d, o_n, mod, w_out, g2, w1, w2, gf)


def _rope_tables(s):
    pos = np.arange(s)
    inv = ROPE_BASE ** (-np.arange(ROPE_FREQS, dtype=np.float32) / ROPE_FREQS)
    ang_r = (pos // GRID_W).astype(np.float32)[:, None] * inv
    ang_c = (pos % GRID_W).astype(np.float32)[:, None] * inv
    cos = np.concatenate([np.cos(ang_r)] * 2 + [np.cos(ang_c)] * 2, axis=1)
    sin = np.concatenate([-np.sin(ang_r), np.sin(ang_r), -np.sin(ang_c), np.sin(ang_c)], axis=1)
    reps = LANES // HEAD_DIM
    return (jnp.asarray(np.tile(cos, (1, reps)), F32), jnp.asarray(np.tile(sin, (1, reps)), F32))


def _tiles(s):
    return min(1024, s), min(512, s), min(256, s)


def kernel(x, c, ctx, c_ctx, w_mod, b_mod, norm1_g, w_in, lam_q1, lam_k1, lam_q2, lam_k2,
           diff_subln_g, na_rpb, w_out, norm2_g, w_fc1, w_fc2, final_g):
    b, s, d = x.shape
    rows = s // GRID_W
    assert w_mod.shape[0] == 1, "single-layer block"
    assert s % (GRID_W * NA_QROWS) == 0 and rows >= NA_KROWS
    ts_proj, ts_tail, tq = _tiles(s)

    n_cond = b + 1
    pad = (-n_cond) % SUBLANES
    cond = jnp.concatenate([c, c_ctx[None, :], jnp.zeros((pad, d), F32)], axis=0)
    mod = _mod(cond, w_mod[0], b_mod[0][None, :])
    mod_x = mod[:b].reshape(b, 6, d)
    mod_c = mod[b:b + 1].reshape(1, 6, d)

    w_in_b = w_in[0].astype(BF16)
    w_qk, w_v_t = w_in_b[:, :2 * d], w_in_b[:, 2 * d:].T
    cos, sin = _rope_tables(s)
    g1 = norm1_g[0][None, :]
    q, k, vt_d, vt_n = _inproj(x, mod_x, g1, w_qk, w_v_t, cos, sin, ts=ts_proj, has_q=True)
    n_ctx = ctx.shape[1]
    kc, vct_d, vct_n = _inproj(ctx, mod_c, g1, w_qk[:, d:], w_v_t, cos[:n_ctx], sin[:n_ctx],
                               ts=n_ctx, has_q=False)

    lamv = jnp.stack([lam_q1[0], lam_k1[0], lam_q2[0], lam_k2[0]])
    o_d = _diff_attention(q, k, vt_d, kc, vct_d, lamv, diff_subln_g[0][None, :], tq=tq)
    cases, case_ids = _na_block_cases(rows)
    o_n = _natten(q, k, vt_n, kc, vct_n, _na_toeplitz(na_rpb[0]), cases, case_ids)

    return _tail(x, o_d, o_n, mod_x, w_out[0].astype(BF16), norm2_g[0][None, :],
                 w_fc1[0].astype(BF16), w_fc2[0].astype(BF16), final_g[None, :], ts=ts_tail)
```

```python
import functools
import math

import numpy as np
import jax
import jax.numpy as jnp
from jax import lax
from jax.experimental import pallas as pl
from jax.experimental.pallas import tpu as pltpu

GRID_W = 64
HEAD_DIM = 64
DIFF_HEADS = 4
NA_HEADS = 8
NA_KH = 8
NA_KW = 16
ROPE_BASE = 10000.0
ROPE_FREQS = HEAD_DIM // 4
EPS = 1e-6
NEG_INF = -1e30
LAM_INIT = 0.8 - 0.6 * math.exp(-0.3 * 0)
LOG2E = math.log2(math.e)

LANES = 128
SUBLANES = 8
VMEM_LIMIT = 56 * 1024 * 1024

NA_QROWS = 4
NA_KROWS = 12
NA_SLAB = HEAD_DIM + 16
NA_SETS = 4
DIFF_ORDER = (0, 1, 2)
MAX_UNSHIFTED_SCORE = 60.0
SCORE_BOUND_MARGIN = 1.05
NA_ORDER = (3, 0, 1, 2, 4)

BF16 = jnp.bfloat16
F32 = jnp.float32


def _params(*sem):
    return pltpu.CompilerParams(dimension_semantics=sem, vmem_limit_bytes=VMEM_LIMIT)


def _const_spec(shape):
    nd = len(shape)
    return pl.BlockSpec(shape, lambda *_: (0,) * nd, pipeline_mode=pl.Buffered(1))


def _dot(a, b):
    return jnp.dot(a, b, preferred_element_type=F32)


def _dot_nt(a, b):
    return lax.dot_general(a, b, (((1,), (1,)), ((), ())), preferred_element_type=F32)


def _rms(xf):
    return xf * lax.rsqrt(jnp.mean(xf * xf, axis=-1, keepdims=True) + EPS)


def _mod_kernel(cond_ref, w_ref, b_ref, o_ref):
    cnd = cond_ref[...]
    act = (cnd * jax.nn.sigmoid(cnd)).astype(BF16)
    o_ref[...] = _dot(act, w_ref[...].astype(BF16)) + b_ref[...]


def _mod(cond, w_mod, b_mod, tn=768):
    r, d = cond.shape
    n = w_mod.shape[1]
    return pl.pallas_call(
        _mod_kernel,
        out_shape=jax.ShapeDtypeStruct((r, n), F32),
        grid=(n // tn,),
        in_specs=[pl.BlockSpec((r, d), lambda j: (0, 0)),
                  pl.BlockSpec((d, tn), lambda j: (0, j)),
                  pl.BlockSpec((1, tn), lambda j: (0, j))],
        out_specs=pl.BlockSpec((r, tn), lambda j: (0, j)),
        compiler_params=_params("arbitrary"),
        name="mod",
    )(cond, w_mod, b_mod)


def _rope(xf, cos, sin_signed):
    lane = lax.broadcasted_iota(jnp.int32, xf.shape, 1)
    first = (lane % (2 * ROPE_FREQS)) < ROPE_FREQS
    partner = jnp.where(first,
                        pltpu.roll(xf, LANES - ROPE_FREQS, 1),
                        pltpu.roll(xf, ROPE_FREQS, 1))
    return xf * cos + partner * sin_signed


def _inproj_kernel(x_ref, mod_ref, g_ref, w_ref, wt_ref, cos_ref, sin_ref, *out_refs,
                   n_rope, q_scale, has_q, chunk):
    xf = x_ref[0]
    shift = mod_ref[0, 0:1, :]
    scale = mod_ref[0, 1:2, :]
    h = (_rms(xf) * g_ref[...]) * (1.0 + scale) + shift
    hb = h.astype(BF16)
    nat_refs, vtd_ref, vtn_ref = out_refs[:-2], out_refs[-2], out_refs[-1]
    col0 = 0
    for oi, o_ref in enumerate(nat_refs):
        width = o_ref.shape[-1]
        is_q = has_q and oi == 0
        roped = has_q and oi < 2
        for c0 in range(0, width, chunk):
            p = _dot(hb, w_ref[:, col0 + c0:col0 + c0 + chunk])
            for c in range(c0 // LANES, (c0 + chunk) // LANES):
                pc = p[:, c * LANES - c0:(c + 1) * LANES - c0]
                if roped and c < n_rope:
                    pc = _rope(pc, cos_ref[...], sin_ref[...])
                if is_q:
                    pc = pc * q_scale
                o_ref[0, :, c * LANES:(c + 1) * LANES] = pc.astype(o_ref.dtype)
        col0 += width
    n_d = vtd_ref.shape[1]
    vtd_ref[0] = _dot_nt(wt_ref[0:n_d, :], hb).astype(vtd_ref.dtype)
    vn = _dot_nt(wt_ref[n_d:, :], hb)
    for hd in range(NA_HEADS):
        r0 = hd * NA_SLAB
        vtn_ref[0, r0:r0 + HEAD_DIM, :] = vn[hd * HEAD_DIM:(hd + 1) * HEAD_DIM].astype(vtn_ref.dtype)
        vtn_ref[0, r0 + HEAD_DIM:r0 + NA_SLAB, :] = jnp.ones((NA_SLAB - HEAD_DIM, vn.shape[1]),
                                                             vtn_ref.dtype)


def _inproj(x, mod, g, w, wt, cos, sin, *, ts, has_q):
    b, t, d = x.shape
    t_widths = [DIFF_HEADS * 2 * HEAD_DIM, NA_HEADS * NA_SLAB]
    nat_widths = [d] * (w.shape[1] // d)
    per_batch_mod = mod.shape[0] > 1
    kern = functools.partial(_inproj_kernel, n_rope=(DIFF_HEADS if has_q else 0),
                             q_scale=HEAD_DIM ** -0.5 * LOG2E, has_q=has_q, chunk=512)
    return pl.pallas_call(
        kern,
        out_shape=[jax.ShapeDtypeStruct((b, t, wd), BF16) for wd in nat_widths]
        + [jax.ShapeDtypeStruct((b, wd, t), BF16) for wd in t_widths],
        grid=(b, t // ts),
        in_specs=[pl.BlockSpec((1, ts, d), lambda i, j: (i, j, 0)),
                  pl.BlockSpec((1,) + mod.shape[1:],
                               (lambda i, j: (i, 0, 0)) if per_batch_mod else (lambda i, j: (0, 0, 0))),
                  _const_spec(g.shape),
                  _const_spec(w.shape),
                  _const_spec(wt.shape),
                  pl.BlockSpec((ts, LANES), lambda i, j: (j, 0)),
                  pl.BlockSpec((ts, LANES), lambda i, j: (j, 0))],
        out_specs=[pl.BlockSpec((1, ts, wd), lambda i, j: (i, j, 0)) for wd in nat_widths]
        + [pl.BlockSpec((1, wd, ts), lambda i, j: (i, 0, j)) for wd in t_widths],
        compiler_params=_params("parallel", "arbitrary"),
        name="inproj_q" if has_q else "inproj_ctx",
    )(x, mod, g, w, wt, cos, sin)


def _ds(start, size):
    if isinstance(start, int):
        return pl.ds(start, size)
    return pl.ds(pl.multiple_of(start, size), size)


def _run_pipeline(n, stages, n_sets, issue_order=None, static_edge=0):
    k = len(stages)
    issue_order = tuple(range(k)) if issue_order is None else issue_order
    assert sorted(issue_order) == list(range(k))
    assert n % n_sets == 0 and n >= n_sets + k + 2 * static_edge

    def step(t, u):
        for i in issue_order:
            c = t - i
            if isinstance(c, int) and not 0 <= c < n:
                continue
            stages[i](c, (u - i) % n_sets)

    t0 = -(-(static_edge + k - 1) // n_sets) * n_sets
    t1 = t0 + (n - static_edge - t0) // n_sets * n_sets
    for t in range(t0):
        step(t, t % n_sets)

    def body(tt, carry):
        for u in range(n_sets):
            step(tt * n_sets + u, u)
        return carry

    lax.fori_loop(t0 // n_sets, t1 // n_sets, body, 0)
    for t in range(t1, n + k - 1):
        step(t, t % n_sets)


def _diff_kernel(q_ref, k_ref, vt_ref, kc_ref, vct_ref, lamv_ref, g_ref, o_ref,
                 kall, vtall, eb_a, eb_b, *hand_off, tq):
    sets = tuple(tuple(hand_off[i:i + 3]) for i in range(0, len(hand_off), 3))
    ebs = (eb_a, eb_b)
    s = k_ref.shape[1]
    w = 2 * HEAD_DIM
    n_j = s // tq
    for h in range(DIFF_HEADS):
        cols = slice(h * w, (h + 1) * w)
        kall[h, 0:s, :] = k_ref[0, :, cols]
        kall[h, s:, :] = kc_ref[0, :, cols]
        vtall[h, 0:w, 0:s] = vt_ref[0, cols, :]
        vtall[h, 0:w, s:] = vct_ref[0, cols, :]
        vtall[h, w:, :] = jnp.ones((vtall.shape[1] - w, vtall.shape[2]), vtall.dtype)
    lv = lamv_ref[...]
    lam = (jnp.exp(jnp.sum(lv[0:1] * lv[1:2], axis=-1, keepdims=True))
           - jnp.exp(jnp.sum(lv[2:3] * lv[3:4], axis=-1, keepdims=True)) + LAM_INIT)

    def chunk(c):
        h, j = c // n_j, c % n_j
        return h, (_ds(j * tq, tq), _ds(h * w, w))

    def scores(c, p):
        st, mb, _ = sets[p]
        h, idx = chunk(c)
        q = q_ref[(0,) + idx]
        lane = lax.broadcasted_iota(jnp.int32, q.shape, 1)
        for i in range(2):
            sel = (lane < HEAD_DIM) if i == 0 else (lane >= HEAD_DIM)
            qi = jnp.where(sel, q, jnp.zeros_like(q))
            x = _dot_nt(kall[h], qi)
            st[i] = x
            mb[i] = jnp.max(x, axis=0, keepdims=True)

    def values(c, p):
        st, mb, ot = sets[p]
        h, _ = chunk(c)
        for i in range(2):
            e = jnp.exp2(st[i] - mb[i]).astype(BF16)
            ot[i] = _dot(vtall[h], e)

    def finish(c, p):
        _, _, ot = sets[p]
        _, idx = chunk(c)
        o1 = ot[0, 0:w, :] / ot[0, w:w + 1, :]
        o2 = ot[1, 0:w, :] / ot[1, w:w + 1, :]
        o = (o1 - lam * o2).T
        o_ref[(0,) + idx] = (_rms(o) * g_ref[...] * (1.0 - LAM_INIT)).astype(o_ref.dtype)

    def scores_unshifted(c, p):
        h, idx = chunk(c)
        q = q_ref[(0,) + idx]
        lane = lax.broadcasted_iota(jnp.int32, q.shape, 1)
        for i in range(2):
            sel = (lane < HEAD_DIM) if i == 0 else (lane >= HEAD_DIM)
            qi = jnp.where(sel, q, jnp.zeros_like(q))
            ebs[p][i] = jnp.exp2(_dot_nt(kall[h], qi)).astype(BF16)

    def values_unshifted(c, p):
        ot = sets[p][2]
        h, _ = chunk(c)
        for i in range(2):
            ot[i] = _dot(vtall[h], ebs[p][i])

    d_half = q_ref.shape[2]
    grp = (lax.broadcasted_iota(jnp.int32, (d_half, LANES), 0) // HEAD_DIM
           == lax.broadcasted_iota(jnp.int32, (d_half, LANES), 1)).astype(BF16)

    def max_norm2(x):
        return jnp.max(_dot(x * x, grp))

    k_norm2 = jnp.maximum(max_norm2(k_ref[0]), max_norm2(kc_ref[0]))
    bounded = max_norm2(q_ref[0]) * k_norm2 * SCORE_BOUND_MARGIN < MAX_UNSHIFTED_SCORE ** 2

    n_items = DIFF_HEADS * n_j
    lax.cond(bounded,
             lambda: _run_pipeline(n_items, (scores_unshifted, values_unshifted, finish), len(sets)),
             lambda: _run_pipeline(n_items, (scores, values, finish), len(sets), DIFF_ORDER))


def _diff_attention(q, k, vt, kc, vct, lamv, g, *, tq):
    b, s, d = q.shape
    c = kc.shape[1]
    w = 2 * HEAD_DIM
    half = DIFF_HEADS * w
    ones_rows = 2 * SUBLANES
    hand_off = [pltpu.VMEM((2, s + c, tq), F32), pltpu.VMEM((2, 1, tq), F32),
                pltpu.VMEM((2, w + ones_rows, tq), F32)]
    return pl.pallas_call(
        functools.partial(_diff_kernel, tq=tq),
        out_shape=jax.ShapeDtypeStruct((b, s, half), BF16),
        grid=(b,),
        in_specs=[pl.BlockSpec((1, s, half), lambda i: (i, 0, 0)),
                  pl.BlockSpec((1, s, half), lambda i: (i, 0, 0)),
                  pl.BlockSpec((1, half, s), lambda i: (i, 0, 0)),
                  pl.BlockSpec((1, c, half), lambda i: (i, 0, 0)),
                  pl.BlockSpec((1, half, c), lambda i: (i, 0, 0)),
                  pl.BlockSpec(lamv.shape, lambda i: (0, 0)),
                  pl.BlockSpec(g.shape, lambda i: (0, 0))],
        out_specs=pl.BlockSpec((1, s, half), lambda i: (i, 0, 0)),
        scratch_shapes=[pltpu.VMEM((DIFF_HEADS, s + c, w), BF16),
                        pltpu.VMEM((DIFF_HEADS, w + ones_rows, s + c), BF16),
                        pltpu.VMEM((2, s + c, tq), BF16),
                        pltpu.VMEM((2, s + c, tq), BF16)] + hand_off * 2,
        compiler_params=_params("parallel"),
        name="diff_attn",
    )(q, k, vt, kc, vct, lamv, g)


def _natten_kernel(q_ref, k_ref, vt_ref, kc_ref, vct_ref, toep_ref, o_ref,
                   bias_tab, on_buf, *hand_off, rows, cases, case_ids):
    sets = tuple(tuple(hand_off[i:i + 4]) for i in range(0, len(hand_off), 4))
    n_lat = NA_KROWS * GRID_W
    n_q = NA_QROWS * GRID_W
    n_pairs = NA_QROWS // 2
    live = [[[x for x in range(NA_KROWS)
              if max(pat[x * NA_QROWS + 2 * rp], pat[x * NA_QROWS + 2 * rp + 1]) >= 0]
             for rp in range(n_pairs)] for pat in cases]
    common = max(set(case_ids), key=case_ids.count)
    edge_blocks = max(case_ids.index(common), case_ids[::-1].index(common))
    assert all(ci == common for ci in case_ids[edge_blocks:len(case_ids) - edge_blocks])

    def case_of(c):
        return case_ids[c // NA_HEADS] if isinstance(c, int) else common

    def block(x, rp):
        return slice(x * GRID_W, (x + 1) * GRID_W), slice(rp * LANES, (rp + 1) * LANES)

    @pl.when(pl.program_id(0) == 0)
    def _():
        lane = lax.broadcasted_iota(jnp.int32, (GRID_W, LANES), 1)
        masked = jnp.full((GRID_W, LANES), NEG_INF, F32)
        for ci, pat in enumerate(cases):
            for x in range(NA_KROWS):
                for rp in range(NA_QROWS // 2):
                    d0, d1 = pat[x * NA_QROWS + 2 * rp], pat[x * NA_QROWS + 2 * rp + 1]
                    for h in range(NA_HEADS):
                        left = toep_ref[h, d0] if d0 >= 0 else masked
                        right = toep_ref[h, d1] if d1 >= 0 else masked
                        blk = masked if max(d0, d1) < 0 else jnp.where(lane < GRID_W, left, right)
                        bias_tab[ci, h, x * GRID_W:(x + 1) * GRID_W, rp * LANES:(rp + 1) * LANES] = blk

    def item(c):
        rb, h = c // NA_HEADS, c % NA_HEADS
        ustart = rb * NA_QROWS - NA_KH // 2
        if isinstance(c, int):
            k_rows = pl.ds(min(max(ustart, 0), rows - NA_KROWS) * GRID_W, n_lat)
            slab = pl.ds(h * NA_SLAB, NA_SLAB)
        else:
            k_rows = pl.ds(pl.multiple_of(jnp.clip(ustart, 0, rows - NA_KROWS) * GRID_W, n_q), n_lat)
            slab = pl.ds(pl.multiple_of(h * NA_SLAB, 2 * SUBLANES), NA_SLAB)
        return rb, h, k_rows, _ds(rb * n_q, n_q), _ds((h // 2) * LANES, LANES), slab

    def scores(c, p):
        st = sets[p][0]
        rb, h, k_rows, q_rows, pair_cols, _ = item(c)
        qp = q_ref[0, q_rows, pair_cols]
        lane = lax.broadcasted_iota(jnp.int32, qp.shape, 1)
        sel = (lane < HEAD_DIM) if p % 2 == 0 else (lane >= HEAD_DIM)
        qh = jnp.where(sel, qp, jnp.zeros_like(qp))
        x_lat = _dot_nt(k_ref[0, k_rows, pair_cols], qh)
        ci = case_of(c)
        for rp in range(n_pairs):
            for x in live[ci][rp]:
                bk = block(x, rp)
                st[bk] = x_lat[bk] + bias_tab[(ci, h) + bk]
        st[n_lat:, :] = _dot_nt(kc_ref[0, :, pair_cols], qh)

    def col_max(c, p):
        st, mb, _, _ = sets[p]
        ci = case_of(c)
        for rp in range(n_pairs):
            lanes = block(0, rp)[1]
            m = jnp.max(st[n_lat:, lanes], axis=0, keepdims=True)
            for x in live[ci][rp]:
                m = jnp.maximum(m, jnp.max(st[block(x, rp)], axis=0, keepdims=True))
            mb[:, lanes] = m

    def exps(c, p):
        st, mb, eb, _ = sets[p]
        ci = case_of(c)
        for rp in range(n_pairs):
            lanes = block(0, rp)[1]
            m = mb[:, lanes]
            for x in range(NA_KROWS):
                bk = block(x, rp)
                if x in live[ci][rp]:
                    eb[bk] = jnp.exp2((st[bk] - m).astype(BF16))
                else:
                    eb[bk] = jnp.exp2(jnp.full((GRID_W, LANES), NEG_INF, F32) - m).astype(BF16)
            eb[n_lat:, lanes] = jnp.exp2((st[n_lat:, lanes] - m).astype(BF16))

    def values(c, p):
        _, _, eb, ot = sets[p]
        _, _, k_rows, _, _, slab = item(c)
        ot[...] = (_dot(vt_ref[0, slab, k_rows], eb[0:n_lat, :])
                   + _dot(vct_ref[0, slab, :], eb[n_lat:, :]))

    def finish(c, p):
        _, _, _, ot = sets[p]
        _, _, _, q_rows, pair_cols, _ = item(c)
        half = p % 2
        on_buf[half * HEAD_DIM:(half + 1) * HEAD_DIM, :] = (
            ot[0:HEAD_DIM, :] / ot[HEAD_DIM:HEAD_DIM + 1, :])
        if half == 1:
            o_ref[0, q_rows, pair_cols] = on_buf[...].T.astype(o_ref.dtype)

    def scores_unshifted(c, p):
        eb = sets[p][2]
        rb, h, k_rows, q_rows, pair_cols, _ = item(c)
        qp = q_ref[0, q_rows, pair_cols]
        lane = lax.broadcasted_iota(jnp.int32, qp.shape, 1)
        sel = (lane < HEAD_DIM) if p % 2 == 0 else (lane >= HEAD_DIM)
        qh = jnp.where(sel, qp, jnp.zeros_like(qp))
        ci = case_of(c)
        x0 = min(min(rows_) for rows_ in live[ci])
        x1 = max(max(rows_) for rows_ in live[ci]) + 1
        first = k_rows.start + x0 * GRID_W
        if not isinstance(first, int):
            first = pl.multiple_of(first, GRID_W)
        x_lat = _dot_nt(k_ref[0, pl.ds(first, (x1 - x0) * GRID_W), pair_cols], qh)
        for rp in range(n_pairs):
            lanes = block(0, rp)[1]
            for x in range(NA_KROWS):
                bk = block(x, rp)
                if x in live[ci][rp]:
                    xs = x_lat[(x - x0) * GRID_W:(x - x0 + 1) * GRID_W, lanes]
                    eb[bk] = jnp.exp2(xs + bias_tab[(ci, h) + bk]).astype(BF16)
                else:
                    eb[bk] = (x_lat[0:GRID_W, lanes] * 0.0).astype(BF16)
        eb[n_lat:, :] = jnp.exp2(_dot_nt(kc_ref[0, :, pair_cols], qh)).astype(BF16)

    w_na = q_ref.shape[2]
    grp = (lax.broadcasted_iota(jnp.int32, (w_na, LANES), 0) // HEAD_DIM
           == lax.broadcasted_iota(jnp.int32, (w_na, LANES), 1)).astype(BF16)

    def max_norm2(x):
        return jnp.max(_dot(x * x, grp))

    tv = toep_ref[...]
    bias_max = jnp.max(jnp.where(tv > 0.5 * NEG_INF, jnp.abs(tv), 0.0))
    k_norm2 = jnp.maximum(max_norm2(k_ref[0]), max_norm2(kc_ref[0]))
    qk_max = jnp.sqrt(max_norm2(q_ref[0]) * k_norm2) * SCORE_BOUND_MARGIN
    bounded = qk_max + bias_max < MAX_UNSHIFTED_SCORE

    assert len(sets) % 2 == 0
    n_items = (rows // NA_QROWS) * NA_HEADS
    edge = edge_blocks * NA_HEADS
    lax.cond(bounded,
             lambda: _run_pipeline(n_items, (scores_unshifted, values, finish), len(sets),
                                   static_edge=edge),
             lambda: _run_pipeline(n_items, (scores, col_max, exps, values, finish), len(sets),
                                   NA_ORDER, static_edge=edge))


def _na_block_patterns(rows):
    pats = []
    for rb in range(rows // NA_QROWS):
        ustart = int(np.clip(rb * NA_QROWS - NA_KH // 2, 0, rows - NA_KROWS))
        pat = []
        for x in range(NA_KROWS):
            for rr in range(NA_QROWS):
                r = rb * NA_QROWS + rr
                start = int(np.clip(r - NA_KH // 2, 0, rows - NA_KH))
                a = ustart + x
                pat.append(a - r + NA_KH - 1 if start <= a < start + NA_KH else -1)
        pats.append(tuple(pat))
    return pats


def _na_block_cases(rows):
    pats = _na_block_patterns(rows)
    cases = tuple(sorted(set(pats), key=pats.index))
    return cases, tuple(cases.index(p) for p in pats)


def _na_toeplitz(rpb):
    jq = np.arange(GRID_W)
    col_start = np.clip(jq - NA_KW // 2, 0, GRID_W - NA_KW)
    in_win = (jq[None, :] >= col_start[:, None]) & (jq[None, :] < col_start[:, None] + NA_KW)
    dc_idx = np.clip(jq[None, :] - jq[:, None] + NA_KW - 1, 0, 2 * NA_KW - 2)
    onehot = (dc_idx.T[None] == np.arange(2 * NA_KW - 1)[:, None, None]) & in_win.T[None]
    toep = jnp.einsum("hdj,jkq->hdkq", rpb, jnp.asarray(onehot, F32),
                      precision=lax.Precision.HIGHEST)
    toep = jnp.where(jnp.asarray(in_win.T)[None, None], toep * LOG2E, NEG_INF)
    return jnp.concatenate([toep, toep], axis=-1)


def _natten(q, k, vt, kc, vct, toep, cases, case_ids):
    b, s, d = q.shape
    c = kc.shape[1]
    rows = s // GRID_W
    w = NA_HEADS * HEAD_DIM
    n_q = NA_QROWS * GRID_W
    n_keys = NA_KROWS * GRID_W + c
    hand_off = [pltpu.VMEM((n_keys, n_q), F32), pltpu.VMEM((1, n_q), F32),
                pltpu.VMEM((n_keys, n_q), BF16),
                pltpu.VMEM((NA_SLAB, n_q), F32)]
    return pl.pallas_call(
        functools.partial(_natten_kernel, rows=rows, cases=cases, case_ids=case_ids),
        out_shape=jax.ShapeDtypeStruct((b, s, w), BF16),
        grid=(b,),
        in_specs=[pl.BlockSpec((1, s, w), lambda i: (i, 0, 1)),
                  pl.BlockSpec((1, s, w), lambda i: (i, 0, 1)),
                  pl.BlockSpec((1,) + vt.shape[1:], lambda i: (i, 0, 0)),
                  pl.BlockSpec((1, c, w), lambda i: (i, 0, 1)),
                  pl.BlockSpec((1,) + vct.shape[1:], lambda i: (i, 0, 0)),
                  _const_spec(toep.shape)],
        out_specs=pl.BlockSpec((1, s, w), lambda i: (i, 0, 0)),
        scratch_shapes=[pltpu.VMEM((len(cases), NA_HEADS, NA_KROWS * GRID_W, n_q), F32),
                        pltpu.VMEM((LANES, n_q), F32)] + hand_off * NA_SETS,
        compiler_params=_params("arbitrary"),
        name="natten",
    )(q, k, vt, kc, vct, toep)


def _tail_kernel(x_ref, od_ref, on_ref, mod_ref, wo_ref, g2_ref, w1_ref, w2_ref, gf_ref, o_ref,
                 *, ff_chunk):
    half = od_ref.shape[-1]
    gate_a = mod_ref[0, 2:3, :]
    shift_m = mod_ref[0, 3:4, :]
    scale_m = mod_ref[0, 4:5, :]
    gate_m = mod_ref[0, 5:6, :]
    attn = _dot(od_ref[0], wo_ref[:half, :]) + _dot(on_ref[0], wo_ref[half:, :])
    x1 = x_ref[0] + gate_a * attn
    h2 = ((_rms(x1) * g2_ref[...]) * (1.0 + scale_m) + shift_m).astype(BF16)
    y = None
    for c in range(w1_ref.shape[1] // ff_chunk):
        cols = slice(c * ff_chunk, (c + 1) * ff_chunk)
        a = jnp.maximum(_dot(h2, w1_ref[:, cols]), 0.0)
        part = _dot((a * a).astype(BF16), w2_ref[cols, :])
        y = part if y is None else y + part
    x2 = x1 + gate_m * y
    o_ref[0] = _rms(x2) * gf_ref[...]


def _tail(x, o_d, o_n, mod, w_out, g2, w1, w2, gf, *, ts, ff_chunk=1024):
    b, s, d = x.shape
    half = o_d.shape[-1]
    return pl.pallas_call(
        functools.partial(_tail_kernel, ff_chunk=ff_chunk),
        out_shape=jax.ShapeDtypeStruct((b, s, d), F32),
        grid=(b, s // ts),
        in_specs=[pl.BlockSpec((1, ts, d), lambda i, j: (i, j, 0)),
                  pl.BlockSpec((1, ts, half), lambda i, j: (i, j, 0)),
                  pl.BlockSpec((1, ts, half), lambda i, j: (i, j, 0)),
                  pl.BlockSpec((1,) + mod.shape[1:], lambda i, j: (i, 0, 0)),
                  _const_spec(w_out.shape),
                  _const_spec(g2.shape),
                  _const_spec(w1.shape),
                  _const_spec(w2.shape),
                  _const_spec(gf.shape)],
        out_specs=pl.BlockSpec((1, ts, d), lambda i, j: (i, j, 0)),
        compiler_params=_params("parallel", "arbitrary"),
        name="tail",
    )(x, o_d, o_n, mod, w_out, g2, w1, w2, gf)


def _rope_tables(s):
    pos = np.arange(s)
    inv = ROPE_BASE ** (-np.arange(ROPE_FREQS, dtype=np.float32) / ROPE_FREQS)
    ang_r = (pos // GRID_W).astype(np.float32)[:, None] * inv
    ang_c = (pos % GRID_W).astype(np.float32)[:, None] * inv
    cos = np.concatenate([np.cos(ang_r)] * 2 + [np.cos(ang_c)] * 2, axis=1)
    sin = np.concatenate([-np.sin(ang_r), np.sin(ang_r), -np.sin(ang_c), np.sin(ang_c)], axis=1)
    reps = LANES // HEAD_DIM
    return (jnp.asarray(np.tile(cos, (1, reps)), F32), jnp.asarray(np.tile(sin, (1, reps)), F32))


def _tiles(s):
    return min(1024, s), min(512, s), min(256, s)


def kernel(x, c, ctx, c_ctx, w_mod, b_mod, norm1_g, w_in, lam_q1, lam_k1, lam_q2, lam_k2,
           diff_subln_g, na_rpb, w_out, norm2_g, w_fc1, w_fc2, final_g):
    b, s, d = x.shape
    rows = s // GRID_W
    assert w_mod.shape[0] == 1, "single-layer block"
    assert s % (GRID_W * NA_QROWS) == 0 and rows >= NA_KROWS
    ts_proj, ts_tail, tq = _tiles(s)

    n_cond = b + 1
    pad = (-n_cond) % SUBLANES
    cond = jnp.concatenate([c, c_ctx[None, :], jnp.zeros((pad, d), F32)], axis=0)
    mod = _mod(cond, w_mod[0], b_mod[0][None, :])
    mod_x = mod[:b].reshape(b, 6, d)
    mod_c = mod[b:b + 1].reshape(1, 6, d)

    w_in_b = w_in[0].astype(BF16)
    w_qk, w_v_t = w_in_b[:, :2 * d], w_in_b[:, 2 * d:].T
    cos, sin = _rope_tables(s)
    g1 = norm1_g[0][None, :]
    q, k, vt_d, vt_n = _inproj(x, mod_x, g1, w_qk, w_v_t, cos, sin, ts=ts_proj, has_q=True)
    n_ctx = ctx.shape[1]
    kc, vct_d, vct_n = _inproj(ctx, mod_c, g1, w_qk[:, d:], w_v_t, cos[:n_ctx], sin[:n_ctx],
                               ts=n_ctx, has_q=False)

    lamv = jnp.stack([lam_q1[0], lam_k1[0], lam_q2[0], lam_k2[0]])
    o_d = _diff_attention(q, k, vt_d, kc, vct_d, lamv, diff_subln_g[0][None, :], tq=tq)
    cases, case_ids = _na_block_cases(rows)
    o_n = _natten(q, k, vt_n, kc, vct_n, _na_toeplitz(na_rpb[0]), cases, case_ids)

    return _tail(x, o_d, o_n, mod_x, w_out[0].astype(BF16), norm2_g[0][None, :],
                 w_fc1[0].astype(BF16), w_fc2[0].astype(BF16), final_g[None, :], ts=ts_tail)
```
